```python
import jax, jax.numpy as jnp
from jax import lax
import numpy as np

D_MODEL = 2048
BATCH = 1
SEQ = 8192
DEPTH = 4
DEC_BATCH = 4
DEC_SEQ = 4096
PAST_LEN = 128

N_META = 16
LEAD = 128
PAD = LEAD - N_META
D_MIX = D_MODEL
CONV_W = D_MIX // 4
ATT_HD = 128
ATT_HQ = 6
ATT_KV = 2
ATT_G = ATT_HQ // ATT_KV
ATT_Q = ATT_HQ * ATT_HD
ATT_KVW = ATT_KV * ATT_HD
WINDOW = 128
BLK = 128
DN_H = 6
DN_DK = 128
DN_DV = 128
DN_W = DN_H * DN_DV
DN_CHUNK = 64
EPS = 1e-6
SIZES = (CONV_W, CONV_W, CONV_W, CONV_W,
         ATT_Q, ATT_KVW, ATT_KVW, ATT_Q,
         DN_H * DN_DK, DN_H * DN_DK, DN_W, DN_W,
         2 * DN_H, 2 * DN_H)
N_PROJ = 4 * CONV_W + 2 * ATT_Q + 2 * ATT_KVW + 2 * DN_H * DN_DK + 2 * DN_W + 4 * DN_H

kernel_name = "hymba_conv_swa_gdn_bidir_encoder"


def _rmsnorm(x, w):
    xf = x.astype(jnp.float32)
    y = xf * lax.rsqrt(jnp.mean(xf * xf, axis=-1, keepdims=True) + EPS)
    return (y * w.astype(jnp.float32)).astype(x.dtype)


def _l2norm(x):
    return x * lax.rsqrt(jnp.sum(x * x, axis=-1, keepdims=True) + EPS)


def _conv3(x, w):
    xp = jnp.pad(x, ((0, 0), (1, 1), (0, 0)))
    return xp[:, :-2] * w[0] + xp[:, 1:-1] * w[1] + xp[:, 2:] * w[2]


def _window_attn(q, k, v, sink):
    B_, T = q.shape[0], q.shape[1]
    nb = T // BLK
    qb = q.reshape(B_, nb, BLK, ATT_KV, ATT_G, ATT_HD)

    def band(a):
        ab = a.reshape(B_, nb, BLK, ATT_KV, ATT_HD)
        ap = jnp.pad(ab, ((0, 0), (1, 1), (0, 0), (0, 0), (0, 0)))
        return jnp.concatenate([ap[:, :-2], ap[:, 1:-1], ap[:, 2:]], axis=2)

    kb, vb = band(k), band(v)
    s = jnp.einsum('bnqkgd,bnskd->bnkgqs', qb, kb).astype(jnp.float32) * (ATT_HD ** -0.5)
    qpos = jnp.arange(T).reshape(nb, BLK)
    kpos = (jnp.arange(nb)[:, None] - 1) * BLK + jnp.arange(3 * BLK)[None, :]
    dist = jnp.abs(qpos[:, :, None] - kpos[:, None, :])
    allowed = (dist <= WINDOW) & ((kpos >= PAD) & (kpos < T))[:, None, :]
    slopes = jnp.exp2(-8.0 * jnp.arange(1, ATT_HQ + 1, dtype=jnp.float32) / ATT_HQ).reshape(ATT_KV, ATT_G)
    s = s - slopes[None, None, :, :, None, None] * dist.astype(jnp.float32)[None, :, None, None]
    s = jnp.where(allowed[None, :, None, None], s, -jnp.inf)
    sk = sink.astype(jnp.float32).reshape(ATT_KV, ATT_G)[None, None, :, :, None, None]
    m = jnp.maximum(jnp.max(s, axis=-1, keepdims=True), sk)
    p = jnp.exp(s - m)
    p = p / (jnp.sum(p, axis=-1, keepdims=True) + jnp.exp(sk - m))
    o = jnp.einsum('bnkgqs,bnskd->bnqkgd', p.astype(v.dtype), vb)
    return o.reshape(B_, T, ATT_HQ, ATT_HD)


def _gated_delta_chunked(q, k, v, beta, g):
    B_, T, H, DK = q.shape
    DV = v.shape[-1]
    C = DN_CHUNK
    n = T // C

    def ch(a):
        return jnp.moveaxis(a.reshape(B_, n, C, H, -1), 3, 1)

    q = ch(q) * (DK ** -0.5)
    k = ch(k)
    v = ch(v)
    beta = jnp.moveaxis(beta.reshape(B_, n, C, H), 3, 1)
    gc = jnp.cumsum(jnp.moveaxis(g.reshape(B_, n, C, H), 3, 1), axis=-1)
    idx = jnp.arange(C)
    incl = idx[:, None] >= idx[None, :]
    strict = idx[:, None] > idx[None, :]
    decay = jnp.exp(jnp.where(incl, gc[..., :, None] - gc[..., None, :], -jnp.inf))
    kb = k * beta[..., None]
    a_mat = jnp.where(strict, jnp.einsum('bhncd,bhnsd->bhncs', kb, k) * decay, 0.0)
    lhs = a_mat + jnp.eye(C, dtype=jnp.float32)
    rhs = jnp.concatenate([v * beta[..., None], kb * jnp.exp(gc)[..., None]], axis=-1)
    sol = lax.linalg.triangular_solve(lhs, rhs, left_side=True, lower=True)
    u, w = sol[..., :DV], sol[..., DV:]
    qk = jnp.einsum('bhncd,bhnsd->bhncs', q, k) * decay
    q_dec = q * jnp.exp(gc)[..., None]
    g_last = gc[..., -1]
    k_tail = k * jnp.exp(g_last[..., None] - gc)[..., None]

    def step(S, xs):
        q_i, qk_i, u_i, w_i, kt_i, gl_i = xs
        v_new = u_i - jnp.einsum('bhcd,bhdv->bhcv', w_i, S)
        o_i = jnp.einsum('bhcd,bhdv->bhcv', q_i, S) + jnp.einsum('bhcs,bhsv->bhcv', qk_i, v_new)
        S = S * jnp.exp(gl_i)[..., None, None] + jnp.einsum('bhcd,bhcv->bhdv', kt_i, v_new)
        return S, o_i

    xs = tuple(jnp.moveaxis(a, 2, 0) for a in (q_dec, qk, u, w, k_tail, g_last))
    S0 = jnp.zeros((B_, H, DK, DV), jnp.float32)
    _, o = lax.scan(step, S0, xs)
    return jnp.transpose(o, (1, 0, 3, 2, 4)).reshape(B_, T, H, DV)


def _layer(h, valid, norm_w, w_in, conv_a_w, attn_sink, dn_conv_w, dn_a_log, dn_dt_bias, dn_norm_w, w_out):
    B_, T = h.shape[0], h.shape[1]
    xn = _rmsnorm(h, norm_w)
    proj = jnp.einsum('btd,dp->btp', xn, w_in)
    split_idx = [int(i) for i in np.cumsum(SIZES)[:-1]]
    cx, cb, cc, cz, aq, ak, av, az, dq, dk, dv, dz, dbeta, da = jnp.split(proj, split_idx, axis=-1)

    y_conv = cb * _conv3(cc * cx, conv_a_w) * jax.nn.silu(cz)

    y_att = _window_attn(aq.reshape(B_, T, ATT_HQ, ATT_HD), ak.reshape(B_, T, ATT_KV, ATT_HD),
                         av.reshape(B_, T, ATT_KV, ATT_HD), attn_sink)
    y_att = y_att.reshape(B_, T, ATT_Q) * jax.nn.silu(az)

    qkv = jax.nn.silu(_conv3(jnp.concatenate([dq, dk, dv], axis=-1), dn_conv_w)).astype(jnp.float32)
    q = _l2norm(qkv[..., :DN_H * DN_DK].reshape(B_, T, DN_H, DN_DK))
    k = _l2norm(qkv[..., DN_H * DN_DK:2 * DN_H * DN_DK].reshape(B_, T, DN_H, DN_DK))
    k = k * valid.astype(jnp.float32)[None, :, None, None]
    v = qkv[..., 2 * DN_H * DN_DK:].reshape(B_, T, DN_H, DN_DV)
    beta = jax.nn.sigmoid(dbeta.astype(jnp.float32)).reshape(B_, T, 2, DN_H)
    g = -jnp.exp(dn_a_log.astype(jnp.float32)) * jax.nn.softplus(
        da.astype(jnp.float32).reshape(B_, T, 2, DN_H) + dn_dt_bias.astype(jnp.float32))
    o_f = _gated_delta_chunked(q, k, v, beta[:, :, 0], g[:, :, 0])
    fl = lambda a: jnp.flip(a, axis=1)
    o_b = fl(_gated_delta_chunked(fl(q), fl(k), fl(v), fl(beta[:, :, 1]), fl(g[:, :, 1])))
    o_dn = _rmsnorm(o_f + o_b, dn_norm_w).reshape(B_, T, DN_W).astype(h.dtype)
    y_dn = o_dn * jax.nn.silu(dz)

    y = jnp.concatenate([y_conv, y_att, y_dn], axis=-1) * valid[None, :, None]
    return h + jnp.einsum('btm,md->btd', y, w_out)


def _trunk(x, meta_tokens, norm_w, w_in, conv_a_w, attn_sink, dn_conv_w, dn_a_log, dn_dt_bias,
           dn_norm_w, w_out, final_norm_w):
    B_ = x.shape[0]
    lead = jnp.concatenate([jnp.zeros((PAD, D_MODEL), x.dtype), meta_tokens.astype(x.dtype)], axis=0)
    h = jnp.concatenate([jnp.broadcast_to(lead[None], (B_, LEAD, D_MODEL)), x], axis=1)
    T = h.shape[1]
    valid = (jnp.arange(T) >= PAD).astype(x.dtype)
    for l in range(DEPTH):
        h = _layer(h, valid, norm_w[l], w_in[l], conv_a_w[l], attn_sink[l], dn_conv_w[l],
                   dn_a_log[l], dn_dt_bias[l], dn_norm_w[l], w_out[l])
    return _rmsnorm(h, final_norm_w)[:, LEAD:]


def setup_inputs(seed: int = 0) -> dict:
    key = jax.random.key(seed)
    ks = jax.random.split(key, 16)
    f32 = jnp.float32
    dt = jnp.exp(jax.random.uniform(ks[9], (DEPTH, 2, DN_H), f32, np.log(1e-3), np.log(1e-1)))
    return {
        "x_prompt": jax.random.normal(ks[0], (BATCH, SEQ, D_MODEL), f32),
        "x_sample": jax.random.normal(ks[1], (DEC_BATCH, DEC_SEQ, D_MODEL), f32),
        "meta_tokens": jax.random.normal(ks[2], (N_META, D_MODEL), f32),
        "norm_w": 1.0 + 0.02 * jax.random.normal(ks[3], (DEPTH, D_MODEL), f32),
        "w_in": jax.random.normal(ks[4], (DEPTH, D_MODEL, N_PROJ), f32) * (D_MODEL ** -0.5),
        "conv_a_w": jax.random.normal(ks[5], (DEPTH, 3, CONV_W), f32) * (3 ** -0.5),
        "attn_sink": 0.5 * jax.random.normal(ks[6], (DEPTH, ATT_HQ), f32),
        "dn_conv_w": jax.random.normal(ks[7], (DEPTH, 3, 2 * DN_H * DN_DK + DN_W), f32) * (3 ** -0.5),
        "dn_a_log": jnp.log(jax.random.uniform(ks[8], (DEPTH, 2, DN_H), f32, 1.0, 16.0)),
        "dn_dt_bias": dt + jnp.log(-jnp.expm1(-dt)),
        "dn_norm_w": 1.0 + 0.02 * jax.random.normal(ks[10], (DEPTH, DN_DV), f32),
        "w_out": jax.random.normal(ks[11], (DEPTH, D_MIX, D_MODEL), f32) * (0.5 * D_MIX ** -0.5),
        "final_norm_w": 1.0 + 0.02 * jax.random.normal(ks[12], (D_MODEL,), f32),
    }


def reference(x_prompt, x_sample, meta_tokens, norm_w, w_in, conv_a_w, attn_sink, dn_conv_w,
              dn_a_log, dn_dt_bias, dn_norm_w, w_out, final_norm_w):
    y_prompt = _trunk(x_prompt, meta_tokens, norm_w, w_in, conv_a_w, attn_sink, dn_conv_w,
                      dn_a_log, dn_dt_bias, dn_norm_w, w_out, final_norm_w)
    y_sample = _trunk(x_sample, meta_tokens, norm_w, w_in, conv_a_w, attn_sink, dn_conv_w,
                      dn_a_log, dn_dt_bias, dn_norm_w, w_out, final_norm_w)
    return (y_prompt, y_sample)
```

```python
import functools

import numpy as np
import jax
import jax.numpy as jnp
from jax import lax
from jax.experimental import pallas as pl
from jax.experimental.pallas import tpu as pltpu

F32 = jnp.float32
BF16 = jnp.bfloat16
HIGHEST = lax.Precision.HIGHEST

D_MODEL = 2048
N_META = 16
BLK = 128
PAD = BLK - N_META
CONV_W = 512
HD = 128
ATT_HQ = 6
ATT_KV = 2
ATT_G = ATT_HQ // ATT_KV
ATT_Q = ATT_HQ * HD
ATT_KVW = ATT_KV * HD
DN_H = 6
DN_W = DN_H * HD
N_BIG = 4 * CONV_W + 2 * ATT_Q + 2 * ATT_KVW + 4 * DN_W
N_GATE = 4 * DN_H
EPS = 1e-6

CB_AQ, CB_AZ, CB_DQ, CB_DK, CB_DV, CB_DZ = 0, 1, 2, 3, 4, 5
CB_AK, CB_AV = 18, 19
CB_CX, CB_CB, CB_CC, CB_CZ = 10, 11, 12, 13

TM_PROJ = 1024
TN_PROJ = 1024
TM = 512
HALO = 8
VMEM_LIMIT = 56 * 1024 * 1024


def _cparams(*sem):
    return pltpu.CompilerParams(dimension_semantics=sem, vmem_limit_bytes=VMEM_LIMIT)


def _silu(x):
    return x * (1.0 / (1.0 + jnp.exp(-x)))


def _dot(a, b):
    return jnp.dot(a, b, preferred_element_type=F32)


def _dot_nt(a, b, precision=None):
    return lax.dot_general(a, b, (((1,), (1,)), ((), ())), preferred_element_type=F32,
                           precision=precision)


def _in_proj_body(h_ref, nw_ref, w_ref, o_ref, xn_ref):
    @pl.when(pl.program_id(1) == 0)
    def _():
        rows = 256
        for r in range(0, TM_PROJ, rows):
            x = h_ref[r:r + rows, :]
            ms = jnp.mean(x * x, axis=-1, keepdims=True)
            xn_ref[r:r + rows, :] = (x * lax.rsqrt(ms + EPS) * nw_ref[...]).astype(BF16)

    o_ref[...] = _dot(xn_ref[...], w_ref[...]).astype(BF16)


def _in_proj(h, norm_w, w_big, layer):
    n = h.shape[0]
    return pl.pallas_call(
        _in_proj_body,
        grid=(n // TM_PROJ, N_BIG // TN_PROJ),
        in_specs=[
            pl.BlockSpec((TM_PROJ, D_MODEL), lambda i, j: (i, 0)),
            pl.BlockSpec((None, 1, D_MODEL), lambda i, j: (layer, 0, 0)),
            pl.BlockSpec((None, D_MODEL, TN_PROJ), lambda i, j: (layer, 0, j)),
        ],
        out_specs=pl.BlockSpec((TM_PROJ, TN_PROJ), lambda i, j: (i, j)),
        out_shape=jax.ShapeDtypeStruct((n, N_BIG), BF16),
        scratch_shapes=[pltpu.VMEM((TM_PROJ, D_MODEL), BF16)],
        compiler_params=_cparams("parallel", "arbitrary"),
        name="in_proj",
    )(h, norm_w, w_big)


def _softplus(x):
    return jnp.maximum(x, 0.0) + jnp.log1p(jnp.exp(-jnp.abs(x)))


def _gate_fn(x, idx, a_log, bias):
    beta = 1.0 / (1.0 + jnp.exp(-x))
    g = -jnp.exp(a_log) * _softplus(x + bias)
    return jnp.where(idx < 2 * DN_H, beta, jnp.where(idx < N_GATE, g, 0.0))


def _gates_body(h_ref, nw_ref, w_ref, wt_ref, arow_ref, brow_ref, acol_ref, bcol_ref, g_ref, gt_ref):
    x = h_ref[...]
    ms = jnp.mean(x * x, axis=-1, keepdims=True)
    xn = x * lax.rsqrt(ms + EPS) * nw_ref[...]
    p = jnp.dot(xn, w_ref[...], preferred_element_type=F32, precision=HIGHEST)
    pt = _dot_nt(wt_ref[...], xn, precision=HIGHEST)
    g_ref[...] = _gate_fn(p, lax.broadcasted_iota(jnp.int32, p.shape, 1), arow_ref[...], brow_ref[...])
    gt_ref[...] = _gate_fn(pt, lax.broadcasted_iota(jnp.int32, pt.shape, 0), acol_ref[...], bcol_ref[...])


def _gates(h, norm_w, w_small, w_small_t, arow, brow, acol, bcol, layer):
    n = h.shape[0]
    lay3 = lambda i: (layer, 0, 0)
    return pl.pallas_call(
        _gates_body,
        grid=(n // TM,),
        in_specs=[
            pl.BlockSpec((TM, D_MODEL), lambda i: (i, 0)),
            pl.BlockSpec((None, 1, D_MODEL), lay3),
            pl.BlockSpec((None, D_MODEL, BLK), lay3),
            pl.BlockSpec((None, BLK, D_MODEL), lay3),
            pl.BlockSpec((None, 1, BLK), lay3),
            pl.BlockSpec((None, 1, BLK), lay3),
            pl.BlockSpec((None, BLK, 1), lay3),
            pl.BlockSpec((None, BLK, 1), lay3),
        ],
        out_specs=[pl.BlockSpec((TM, BLK), lambda i: (i, 0)),
                   pl.BlockSpec((BLK, TM), lambda i: (0, i))],
        out_shape=[jax.ShapeDtypeStruct((n, BLK), F32), jax.ShapeDtypeStruct((BLK, n), F32)],
        compiler_params=_cparams("parallel"),
        name="gates",
    )(h, norm_w, w_small, w_small_t, arow, brow, acol, bcol)


def _halo_specs(width, col_block, n):
    last = n // HALO - 1
    per = TM // HALO
    return [
        pl.BlockSpec((TM, width), lambda i: (i, col_block)),
        pl.BlockSpec((HALO, width), lambda i: (jnp.maximum(i * per - 1, 0), col_block)),
        pl.BlockSpec((HALO, width), lambda i: (jnp.minimum((i + 1) * per, last), col_block)),
    ]


def _shifted(buf_ref, cur, prev, nxt):
    i = pl.program_id(0)
    has_prev = (i > 0).astype(F32)
    has_next = (i < pl.num_programs(0) - 1).astype(F32)
    buf_ref[0:HALO, :] = prev * has_prev
    buf_ref[HALO:HALO + TM, :] = cur
    buf_ref[HALO + TM:2 * HALO + TM, :] = nxt * has_next
    return buf_ref[HALO - 1:HALO - 1 + TM, :], buf_ref[HALO + 1:HALO + 1 + TM, :]


def _conv_body(cx, cxp, cxn, cc, ccp, ccn, cb, cz, w_ref, o_ref, buf_ref):
    u = cc[...].astype(F32) * cx[...].astype(F32)
    up = ccp[...].astype(F32) * cxp[...].astype(F32)
    un = ccn[...].astype(F32) * cxn[...].astype(F32)
    u_m1, u_p1 = _shifted(buf_ref, u, up, un)
    conv = u_m1 * w_ref[0:1, :] + u * w_ref[1:2, :] + u_p1 * w_ref[2:3, :]
    o_ref[...] = (cb[...].astype(F32) * conv * _silu(cz[...].astype(F32))).astype(BF16)


def _conv_branch(proj, conv_w, layer):
    n = proj.shape[0]
    return pl.pallas_call(
        _conv_body,
        grid=(n // TM,),
        in_specs=_halo_specs(CONV_W, CB_CX, n) + _halo_specs(CONV_W, CB_CC, n) + [
            pl.BlockSpec((TM, CONV_W), lambda i: (i, CB_CB)),
            pl.BlockSpec((TM, CONV_W), lambda i: (i, CB_CZ)),
            pl.BlockSpec((None, 3, CONV_W), lambda i: (layer, 0, 0)),
        ],
        out_specs=pl.BlockSpec((TM, CONV_W), lambda i: (i, 0)),
        out_shape=jax.ShapeDtypeStruct((n, CONV_W), BF16),
        scratch_shapes=[pltpu.VMEM((TM + 2 * HALO, CONV_W), F32)],
        compiler_params=_cparams("parallel"),
        name="conv_branch",
    )(proj, proj, proj, proj, proj, proj, proj, proj, conv_w)


def _attn_body(pos_ref, nblk_ref, sink_ref, q_ref, az_ref, kp, kc, kn, vp, vc, vn, o_ref):
    b = pl.program_id(0)
    qi = lax.broadcasted_iota(jnp.int32, (BLK, 3 * BLK), 0)
    kj = lax.broadcasted_iota(jnp.int32, (BLK, 3 * BLK), 1) - BLK
    dist = jnp.abs(qi - kj)
    kabs = pos_ref[b] * BLK + kj
    allowed = (dist <= BLK) & (kabs >= PAD) & (kabs < nblk_ref[b] * BLK)
    distf = dist.astype(F32)
    for kvh in range(ATT_KV):
        ks = slice(kvh * HD, (kvh + 1) * HD)
        k3 = jnp.concatenate([kp[:, ks], kc[:, ks], kn[:, ks]], axis=0)
        v3 = jnp.concatenate([vp[:, ks], vc[:, ks], vn[:, ks]], axis=0)
        for g in range(ATT_G):
            head = kvh * ATT_G + g
            hs = slice(head * HD, (head + 1) * HD)
            slope = float(2.0 ** (-8.0 * (head + 1) / ATT_HQ))
            s = _dot_nt(q_ref[:, hs], k3) * (HD ** -0.5) - slope * distf
            s = jnp.where(allowed, s, -jnp.inf)
            sk = sink_ref[head]
            m = jnp.maximum(jnp.max(s, axis=-1, keepdims=True), sk)
            p = jnp.exp(s - m)
            den = jnp.sum(p, axis=-1, keepdims=True) + jnp.exp(sk - m)
            o = _dot(p.astype(BF16), v3) / den
            o_ref[:, hs] = (o * _silu(az_ref[:, hs].astype(F32))).astype(BF16)


def _attention(proj, sink, pos_tab, nblk_tab, layer):
    n = proj.shape[0]
    nb = n // BLK
    prev = lambda b, *_: (jnp.maximum(b - 1, 0),)
    nxt = lambda b, *_: (jnp.minimum(b + 1, nb - 1),)
    kv_specs = [pl.BlockSpec((BLK, ATT_KVW), lambda b, *_, f=f, c=c: f(b) + (c,))
                for c in (CB_AK, CB_AV) for f in (prev, lambda b, *_: (b,), nxt)]
    grid_spec = pltpu.PrefetchScalarGridSpec(
        num_scalar_prefetch=2,
        grid=(nb,),
        in_specs=[
            pl.BlockSpec(memory_space=pltpu.SMEM),
            pl.BlockSpec((BLK, ATT_Q), lambda b, *_: (b, CB_AQ)),
            pl.BlockSpec((BLK, ATT_Q), lambda b, *_: (b, CB_AZ)),
        ] + kv_specs,
        out_specs=pl.BlockSpec((BLK, ATT_Q), lambda b, *_: (b, 0)),
    )
    return pl.pallas_call(
        _attn_body,
        grid_spec=grid_spec,
        out_shape=jax.ShapeDtypeStruct((n, ATT_Q), BF16),
        compiler_params=_cparams("parallel"),
        name="attention",
    )(pos_tab, nblk_tab, sink[layer], proj, proj, proj, proj, proj, proj, proj, proj)


def _l2norm_heads(x):
    parts = []
    for hh in range(DN_H):
        xh = x[:, hh * HD:(hh + 1) * HD]
        parts.append(xh * lax.rsqrt(jnp.sum(xh * xh, axis=-1, keepdims=True) + EPS))
    return parts


def _dn_prep_body(dq, dqp, dqn, dk, dkp, dkn, dv, dvp, dvn, w_ref, valid_ref,
                  q_ref, k_ref, v_ref, buf_ref):
    def conv_silu(cur, prev, nxt, w0):
        x = cur[...].astype(F32)
        x_m1, x_p1 = _shifted(buf_ref, x, prev[...].astype(F32), nxt[...].astype(F32))
        c = (x_m1 * w_ref[0:1, w0:w0 + DN_W] + x * w_ref[1:2, w0:w0 + DN_W]
             + x_p1 * w_ref[2:3, w0:w0 + DN_W])
        return _silu(c)

    q = conv_silu(dq, dqp, dqn, 0)
    for hh, qh in enumerate(_l2norm_heads(q)):
        q_ref[:, hh * HD:(hh + 1) * HD] = qh * (HD ** -0.5)
    k = conv_silu(dk, dkp, dkn, DN_W)
    valid = valid_ref[...]
    for hh, kh in enumerate(_l2norm_heads(k)):
        k_ref[:, hh * HD:(hh + 1) * HD] = kh * valid
    v_ref[...] = conv_silu(dv, dvp, dvn, 2 * DN_W)


def _dn_prep(proj, dn_conv_w, valid, layer):
    n = proj.shape[0]
    out = jax.ShapeDtypeStruct((n, DN_W), F32)
    ospec = pl.BlockSpec((TM, DN_W), lambda i: (i, 0))
    return pl.pallas_call(
        _dn_prep_body,
        grid=(n // TM,),
        in_specs=_halo_specs(DN_W, CB_DQ, n) + _halo_specs(DN_W, CB_DK, n)
        + _halo_specs(DN_W, CB_DV, n) + [
            pl.BlockSpec((None, 3, 3 * DN_W), lambda i: (layer, 0, 0)),
            pl.BlockSpec((TM, 1), lambda i: (i, 0)),
        ],
        out_specs=[ospec, ospec, ospec],
        out_shape=[out, out, out],
        scratch_shapes=[pltpu.VMEM((TM + 2 * HALO, DN_W), F32)],
        compiler_params=_cparams("parallel"),
        name="dn_prep",
    )(proj, proj, proj, proj, proj, proj, proj, proj, proj, dn_conv_w, valid)


def _unit_triangular_inverse(p, xor_idx):
    eye = (xor_idx == 0).astype(F32)
    t = eye + jnp.where(xor_idx == 1, p, 0.0)
    s = 2
    while s < BLK:
        link = jnp.where((xor_idx >= s) & (xor_idx < 2 * s), p, 0.0).astype(BF16)
        tb = t.astype(BF16)
        t = t + _dot(tb, _dot(link, tb).astype(BF16))
        s *= 2
    return t


def _dn_chunk(q, k, v, beta, col, row, tot, incl, strict, xor_idx, s_ref, idx):
    decay = jnp.exp(jnp.where(incl, col - row, -jnp.inf))
    kbeta = k * beta
    gram = _dot_nt(jnp.concatenate([kbeta, q], axis=0).astype(BF16), k.astype(BF16))
    p = -jnp.where(strict, gram[:BLK] * decay, 0.0)
    qk = gram[BLK:] * decay
    ecol = jnp.exp(col)
    rhs = jnp.concatenate([v * beta, kbeta * ecol], axis=1)
    x = _dot(_unit_triangular_inverse(p, xor_idx).astype(BF16), rhs.astype(BF16))
    u, w = x[:, :HD], x[:, HD:]
    s = s_ref[idx]
    ws = _dot(jnp.concatenate([w, q * ecol], axis=0).astype(BF16), s.astype(BF16))
    v_new = u - ws[:BLK]
    k_tail_t = (k * jnp.exp(tot - col)).T
    r = _dot(jnp.concatenate([qk, k_tail_t], axis=0).astype(BF16), v_new.astype(BF16))
    s_ref[idx] = s * jnp.exp(tot) + r[BLK:]
    return ws[BLK:] + r[:BLK]


def _dn_scan_body(fblk_ref, bblk_ref, first_ref,
                  qf, kf, vf, gf, gtf, qb, kb, vb, gb, gtb, of_ref, ob_ref, s_ref):
    i = pl.program_id(0)

    @pl.when(first_ref[i] == 1)
    def _():
        s_ref[...] = jnp.zeros_like(s_ref)

    r_i = lax.broadcasted_iota(jnp.int32, (BLK, BLK), 0)
    c_i = lax.broadcasted_iota(jnp.int32, (BLK, BLK), 1)
    lower = (r_i >= c_i).astype(F32)
    upper = (r_i <= c_i).astype(F32)
    xor_idx = r_i ^ c_i
    for d, (q_ref, k_ref, v_ref, g_ref, gt_ref, o_ref) in enumerate(
            ((qf, kf, vf, gf, gtf, of_ref), (qb, kb, vb, gb, gtb, ob_ref))):
        if d == 0:
            incl, strict, csum, csum_t, last = r_i >= c_i, r_i > c_i, lower, upper, BLK - 1
        else:
            incl, strict, csum, csum_t, last = r_i <= c_i, r_i < c_i, upper, lower, 0
        gates = g_ref[...]
        gc = jnp.dot(csum, gates, preferred_element_type=F32, precision=HIGHEST)
        gct = jnp.dot(gt_ref[...], csum_t, preferred_element_type=F32, precision=HIGHEST)
        for hh in range(DN_H):
            hs = slice(hh * HD, (hh + 1) * HD)
            cb = d * DN_H + hh
            cg = 2 * DN_H + cb
            o_ref[:, hs] = _dn_chunk(
                q_ref[:, hs], k_ref[:, hs], v_ref[:, hs], gates[:, cb:cb + 1],
                gc[:, cg:cg + 1], gct[cg:cg + 1, :], gc[last:last + 1, cg:cg + 1],
                incl, strict, xor_idx, s_ref, cb)


def _dn_scan(q, k, v, gates, gates_t, fblk, bblk, first):
    n = q.shape[0]
    steps = fblk.shape[0]
    fmap = lambda i, f, b, s: (f[i], 0)
    bmap = lambda i, f, b, s: (b[i], 0)
    fmap_t = lambda i, f, b, s: (0, f[i])
    bmap_t = lambda i, f, b, s: (0, b[i])
    wide = lambda m: pl.BlockSpec((BLK, DN_W), m)
    grid_spec = pltpu.PrefetchScalarGridSpec(
        num_scalar_prefetch=3,
        grid=(steps,),
        in_specs=[wide(fmap), wide(fmap), wide(fmap),
                  pl.BlockSpec((BLK, BLK), fmap), pl.BlockSpec((BLK, BLK), fmap_t),
                  wide(bmap), wide(bmap), wide(bmap),
                  pl.BlockSpec((BLK, BLK), bmap), pl.BlockSpec((BLK, BLK), bmap_t)],
        out_specs=[wide(fmap), wide(bmap)],
        scratch_shapes=[pltpu.VMEM((2 * DN_H, HD, HD), F32)],
    )
    out = jax.ShapeDtypeStruct((n, DN_W), F32)
    return pl.pallas_call(
        _dn_scan_body,
        grid_spec=grid_spec,
        out_shape=[out, out],
        compiler_params=_cparams("arbitrary"),
        name="dn_scan",
    )(fblk, bblk, first, q, k, v, gates, gates_t, q, k, v, gates, gates_t)


def _out_proj_body(h_ref, yc_ref, ya_ref, of_ref, ob_ref, dz_ref, nw_ref, valid_ref, w_ref, o_ref):
    parts = []
    for hh in range(DN_H):
        hs = slice(hh * HD, (hh + 1) * HD)
        o = of_ref[:, hs] + ob_ref[:, hs]
        ms = jnp.mean(o * o, axis=-1, keepdims=True)
        on = o * lax.rsqrt(ms + EPS) * nw_ref[...]
        parts.append((on * _silu(dz_ref[:, hs].astype(F32))).astype(BF16))
    y_dn = jnp.concatenate(parts, axis=1)
    c0, c1 = CONV_W, CONV_W + ATT_Q
    acc = (_dot(yc_ref[...], w_ref[0:c0, :]) + _dot(ya_ref[...], w_ref[c0:c1, :])
           + _dot(y_dn, w_ref[c1:, :]))
    o_ref[...] = h_ref[...] + jnp.where(valid_ref[...] > 0.0, acc, 0.0)


def _out_proj(h, y_conv, y_att, o_f, o_b, proj, dn_norm_w, valid, w_out, layer):
    n = h.shape[0]
    row = lambda width: pl.BlockSpec((TM, width), lambda i: (i, 0))
    return pl.pallas_call(
        _out_proj_body,
        grid=(n // TM,),
        in_specs=[row(D_MODEL), row(CONV_W), row(ATT_Q), row(DN_W), row(DN_W),
                  pl.BlockSpec((TM, DN_W), lambda i: (i, CB_DZ)),
                  pl.BlockSpec((None, 1, HD), lambda i: (layer, 0, 0)),
                  row(1),
                  pl.BlockSpec((None, D_MODEL, D_MODEL), lambda i: (layer, 0, 0))],
        out_specs=row(D_MODEL),
        out_shape=jax.ShapeDtypeStruct((n, D_MODEL), F32),
        compiler_params=_cparams("parallel"),
        name="out_proj",
    )(h, y_conv, y_att, o_f, o_b, proj, dn_norm_w, valid, w_out)


def _final_norm_body(h_ref, nw_ref, o_ref):
    x = h_ref[...]
    ms = jnp.mean(x * x, axis=-1, keepdims=True)
    o_ref[...] = x * lax.rsqrt(ms + EPS) * nw_ref[...]


def _final_norm(h, w):
    n = h.shape[0]
    return pl.pallas_call(
        _final_norm_body,
        grid=(n // TM,),
        in_specs=[pl.BlockSpec((TM, D_MODEL), lambda i: (i, 0)),
                  pl.BlockSpec((1, D_MODEL), lambda i: (0, 0))],
        out_specs=pl.BlockSpec((TM, D_MODEL), lambda i: (i, 0)),
        out_shape=jax.ShapeDtypeStruct((n, D_MODEL), F32),
        compiler_params=_cparams("parallel"),
        name="final_norm",
    )(h, w)


def _tables(seq_blocks, n_blocks):
    pos, nblk, fblk, bblk, first = [], [], [], [], []
    start = 0
    for nb in seq_blocks:
        pos += list(range(nb))
        nblk += [nb] * nb
        fblk += [start + j for j in range(nb)]
        bblk += [start + nb - 1 - j for j in range(nb)]
        first += [1] + [0] * (nb - 1)
        start += nb
    pos += [0] * (n_blocks - start)
    nblk += [1] * (n_blocks - start)
    as_i32 = lambda a: jnp.asarray(np.asarray(a, np.int32))
    return as_i32(pos), as_i32(nblk), as_i32(fblk), as_i32(bblk), as_i32(first)


def _forward(xs, meta_tokens, norm_w, w_in, conv_a_w, attn_sink, dn_conv_w, dn_a_log, dn_dt_bias,
             dn_norm_w, w_out, final_norm_w):
    depth = w_in.shape[0]
    lead = jnp.concatenate([jnp.zeros((PAD, D_MODEL), F32), meta_tokens.astype(F32)], axis=0)
    rows, seq_blocks = [], []
    for x in xs:
        for bi in range(x.shape[0]):
            rows += [lead, x[bi]]
            seq_blocks.append((BLK + x.shape[1]) // BLK)
    n_real = sum(seq_blocks) * BLK
    n = -(-n_real // TM_PROJ) * TM_PROJ
    rows.append(jnp.zeros((n - n_real, D_MODEL), F32))
    h = jnp.concatenate(rows, axis=0)

    valid_np = np.zeros((n, 1), np.float32)
    start = 0
    for nb in seq_blocks:
        valid_np[start + PAD:start + nb * BLK] = 1.0
        start += nb * BLK
    valid = jnp.asarray(valid_np)
    pos_tab, nblk_tab, fblk, bblk, first = _tables(seq_blocks, n // BLK)

    c = np.cumsum([0, CONV_W, CONV_W, CONV_W, CONV_W, ATT_Q, ATT_KVW, ATT_KVW, ATT_Q,
                   DN_W, DN_W, DN_W, DN_W, 2 * DN_H, 2 * DN_H])
    w_big = jnp.concatenate([w_in[:, :, c[4]:c[5]], w_in[:, :, c[7]:c[8]], w_in[:, :, c[8]:c[12]],
                             w_in[:, :, c[5]:c[7]], w_in[:, :, c[0]:c[4]]], axis=-1).astype(BF16)
    w_small = jnp.pad(w_in[:, :, c[12]:c[14]], ((0, 0), (0, 0), (0, BLK - N_GATE)))
    w_small_t = jnp.swapaxes(w_small, 1, 2)
    pad_gate = lambda a: jnp.pad(a.reshape(depth, 2 * DN_H), ((0, 0), (2 * DN_H, BLK - N_GATE)))
    a_row = pad_gate(dn_a_log.astype(F32))[:, None, :]
    b_row = pad_gate(dn_dt_bias.astype(F32))[:, None, :]
    a_col, b_col = jnp.swapaxes(a_row, 1, 2), jnp.swapaxes(b_row, 1, 2)
    w_out_b = w_out.astype(BF16)
    norm_w3 = norm_w[:, None, :]
    dn_norm_w3 = dn_norm_w[:, None, :]

    for l in range(depth):
        proj = _in_proj(h, norm_w3, w_big, l)
        gates, gates_t = _gates(h, norm_w3, w_small, w_small_t, a_row, b_row, a_col, b_col, l)
        y_conv = _conv_branch(proj, conv_a_w, l)
        y_att = _attention(proj, attn_sink, pos_tab, nblk_tab, l)
        q, k, v = _dn_prep(proj, dn_conv_w, valid, l)
        o_f, o_b = _dn_scan(q, k, v, gates, gates_t, fblk, bblk, first)
        h = _out_proj(h, y_conv, y_att, o_f, o_b, proj, dn_norm_w3, valid, w_out_b, l)

    out = _final_norm(h, final_norm_w[None, :])
    ys, start = [], 0
    for x in xs:
        b, s = x.shape[0], x.shape[1]
        t = BLK + s
        ys.append(out[start:start + b * t].reshape(b, t, D_MODEL)[:, BLK:])
        start += b * t
    return tuple(ys)


def kernel(x_prompt, x_sample, meta_tokens, norm_w, w_in, conv_a_w, attn_sink, dn_conv_w,
           dn_a_log, dn_dt_bias, dn_norm_w, w_out, final_norm_w):
    return _forward([x_prompt, x_sample], meta_tokens, norm_w, w_in, conv_a_w, attn_sink, dn_conv_w,
                    dn_a_log, dn_dt_bias, dn_norm_w, w_out, final_norm_w)
```

```python
import functools

import numpy as np
import jax
import jax.numpy as jnp
from jax import lax
from jax.experimental import pallas as pl
from jax.experimental.pallas import tpu as pltpu

F32 = jnp.float32
BF16 = jnp.bfloat16
HIGHEST = lax.Precision.HIGHEST

D_MODEL = 2048
N_META = 16
BLK = 128
PAD = BLK - N_META
CONV_W = 512
HD = 128
ATT_HQ = 6
ATT_KV = 2
ATT_G = ATT_HQ // ATT_KV
ATT_Q = ATT_HQ * HD
ATT_KVW = ATT_KV * HD
DN_H = 6
DN_W = DN_H * HD
N_BIG = 4 * CONV_W + 2 * ATT_Q + 2 * ATT_KVW + 4 * DN_W
N_GATE = 4 * DN_H
EPS = 1e-6

CB_AQ, CB_AZ, CB_DQ, CB_DK, CB_DV, CB_DZ = 0, 1, 2, 3, 4, 5
CB_AK, CB_AV = 18, 19
CB_CX, CB_CB, CB_CC, CB_CZ = 10, 11, 12, 13

TM_PROJ = 1024
TN_PROJ = 1024
TM = 512
HALO = 8
VMEM_LIMIT = 56 * 1024 * 1024


def _cparams(*sem):
    return pltpu.CompilerParams(dimension_semantics=sem, vmem_limit_bytes=VMEM_LIMIT)


def _silu(x):
    return x * (1.0 / (1.0 + jnp.exp(-x)))


def _dot(a, b):
    return jnp.dot(a, b, preferred_element_type=F32)


def _dot_nt(a, b, precision=None):
    return lax.dot_general(a, b, (((1,), (1,)), ((), ())), preferred_element_type=F32,
                           precision=precision)


def _in_proj_body(h_ref, nw_ref, w_ref, o_ref, xn_ref):
    @pl.when(pl.program_id(1) == 0)
    def _():
        rows = 256
        for r in range(0, TM_PROJ, rows):
            x = h_ref[r:r + rows, :]
            ms = jnp.mean(x * x, axis=-1, keepdims=True)
            xn_ref[r:r + rows, :] = (x * lax.rsqrt(ms + EPS) * nw_ref[...]).astype(BF16)

    o_ref[...] = _dot(xn_ref[...], w_ref[...]).astype(BF16)


def _in_proj(h, norm_w, w_big, layer):
    n = h.shape[0]
    return pl.pallas_call(
        _in_proj_body,
        grid=(n // TM_PROJ, N_BIG // TN_PROJ),
        in_specs=[
            pl.BlockSpec((TM_PROJ, D_MODEL), lambda i, j: (i, 0)),
            pl.BlockSpec((None, 1, D_MODEL), lambda i, j: (layer, 0, 0)),
            pl.BlockSpec((None, D_MODEL, TN_PROJ), lambda i, j: (layer, 0, j)),
        ],
        out_specs=pl.BlockSpec((TM_PROJ, TN_PROJ), lambda i, j: (i, j)),
        out_shape=jax.ShapeDtypeStruct((n, N_BIG), BF16),
        scratch_shapes=[pltpu.VMEM((TM_PROJ, D_MODEL), BF16)],
        compiler_params=_cparams("parallel", "arbitrary"),
        name="in_proj",
    )(h, norm_w, w_big)


def _softplus(x):
    return jnp.maximum(x, 0.0) + jnp.log1p(jnp.exp(-jnp.abs(x)))


def _gate_fn(x, idx, a_log, bias):
    beta = 1.0 / (1.0 + jnp.exp(-x))
    g = -jnp.exp(a_log) * _softplus(x + bias)
    return jnp.where(idx < 2 * DN_H, beta, jnp.where(idx < N_GATE, g, 0.0))


def _gates_body(h_ref, nw_ref, w_ref, wt_ref, arow_ref, brow_ref, acol_ref, bcol_ref, g_ref, gt_ref):
    x = h_ref[...]
    ms = jnp.mean(x * x, axis=-1, keepdims=True)
    xn = x * lax.rsqrt(ms + EPS) * nw_ref[...]
    p = jnp.dot(xn, w_ref[...], preferred_element_type=F32, precision=HIGHEST)
    pt = _dot_nt(wt_ref[...], xn, precision=HIGHEST)
    g_ref[...] = _gate_fn(p, lax.broadcasted_iota(jnp.int32, p.shape, 1), arow_ref[...], brow_ref[...])
    gt_ref[...] = _gate_fn(pt, lax.broadcasted_iota(jnp.int32, pt.shape, 0), acol_ref[...], bcol_ref[...])


def _gates(h, norm_w, w_small, w_small_t, arow, brow, acol, bcol, layer):
    n = h.shape[0]
    lay3 = lambda i: (layer, 0, 0)
    return pl.pallas_call(
        _gates_body,
        grid=(n // TM,),
        in_specs=[
            pl.BlockSpec((TM, D_MODEL), lambda i: (i, 0)),
            pl.BlockSpec((None, 1, D_MODEL), lay3),
            pl.BlockSpec((None, D_MODEL, BLK), lay3),
            pl.BlockSpec((None, BLK, D_MODEL), lay3),
            pl.BlockSpec((None, 1, BLK), lay3),
            pl.BlockSpec((None, 1, BLK), lay3),
            pl.BlockSpec((None, BLK, 1), lay3),
            pl.BlockSpec((None, BLK, 1), lay3),
        ],
        out_specs=[pl.BlockSpec((TM, BLK), lambda i: (i, 0)),
                   pl.BlockSpec((BLK, TM), lambda i: (0, i))],
        out_shape=[jax.ShapeDtypeStruct((n, BLK), F32), jax.ShapeDtypeStruct((BLK, n), F32)],
        compiler_params=_cparams("parallel"),
        name="gates",
    )(h, norm_w, w_small, w_small_t, arow, brow, acol, bcol)


def _halo_specs(width, col_block, n):
    last = n // HALO - 1
    per = TM // HALO
    return [
        pl.BlockSpec((TM, width), lambda i: (i, col_block)),
        pl.BlockSpec((HALO, width), lambda i: (jnp.maximum(i * per - 1, 0), col_block)),
        pl.BlockSpec((HALO, width), lambda i: (jnp.minimum((i + 1) * per, last), col_block)),
    ]


def _shifted(buf_ref, cur, prev, nxt):
    i = pl.program_id(0)
    has_prev = (i > 0).astype(F32)
    has_next = (i < pl.num_programs(0) - 1).astype(F32)
    buf_ref[0:HALO, :] = prev * has_prev
    buf_ref[HALO:HALO + TM, :] = cur
    buf_ref[HALO + TM:2 * HALO + TM, :] = nxt * has_next
    return buf_ref[HALO - 1:HALO - 1 + TM, :], buf_ref[HALO + 1:HALO + 1 + TM, :]


def _conv_body(cx, cxp, cxn, cc, ccp, ccn, cb, cz, w_ref, o_ref, buf_ref):
    u = cc[...].astype(F32) * cx[...].astype(F32)
    up = ccp[...].astype(F32) * cxp[...].astype(F32)
    un = ccn[...].astype(F32) * cxn[...].astype(F32)
    u_m1, u_p1 = _shifted(buf_ref, u, up, un)
    conv = u_m1 * w_ref[0:1, :] + u * w_ref[1:2, :] + u_p1 * w_ref[2:3, :]
    o_ref[...] = (cb[...].astype(F32) * conv * _silu(cz[...].astype(F32))).astype(BF16)


def _conv_branch(proj, conv_w, layer):
    n = proj.shape[0]
    return pl.pallas_call(
        _conv_body,
        grid=(n // TM,),
        in_specs=_halo_specs(CONV_W, CB_CX, n) + _halo_specs(CONV_W, CB_CC, n) + [
            pl.BlockSpec((TM, CONV_W), lambda i: (i, CB_CB)),
            pl.BlockSpec((TM, CONV_W), lambda i: (i, CB_CZ)),
            pl.BlockSpec((None, 3, CONV_W), lambda i: (layer, 0, 0)),
        ],
        out_specs=pl.BlockSpec((TM, CONV_W), lambda i: (i, 0)),
        out_shape=jax.ShapeDtypeStruct((n, CONV_W), BF16),
        scratch_shapes=[pltpu.VMEM((TM + 2 * HALO, CONV_W), F32)],
        compiler_params=_cparams("parallel"),
        name="conv_branch",
    )(proj, proj, proj, proj, proj, proj, proj, proj, conv_w)


def _attn_body(pos_ref, nblk_ref, sink_ref, q_ref, az_ref, kp, kc, kn, vp, vc, vn, o_ref):
    b = pl.program_id(0)
    qi = lax.broadcasted_iota(jnp.int32, (BLK, 3 * BLK), 0)
    kj = lax.broadcasted_iota(jnp.int32, (BLK, 3 * BLK), 1) - BLK
    dist = jnp.abs(qi - kj)
    kabs = pos_ref[b] * BLK + kj
    allowed = (dist <= BLK) & (kabs >= PAD) & (kabs < nblk_ref[b] * BLK)
    distf = dist.astype(F32)
    for kvh in range(ATT_KV):
        ks = slice(kvh * HD, (kvh + 1) * HD)
        k3 = jnp.concatenate([kp[:, ks], kc[:, ks], kn[:, ks]], axis=0)
        v3 = jnp.concatenate([vp[:, ks], vc[:, ks], vn[:, ks]], axis=0)
        for g in range(ATT_G):
            head = kvh * ATT_G + g
            hs = slice(head * HD, (head + 1) * HD)
            slope = float(2.0 ** (-8.0 * (head + 1) / ATT_HQ))
            s = _dot_nt(q_ref[:, hs], k3) * (HD ** -0.5) - slope * distf
            s = jnp.where(allowed, s, -jnp.inf)
            sk = sink_ref[head]
            m = jnp.maximum(jnp.max(s, axis=-1, keepdims=True), sk)
            p = jnp.exp(s - m)
            den = jnp.sum(p, axis=-1, keepdims=True) + jnp.exp(sk - m)
            o = _dot(p.astype(BF16), v3) / den
            o_ref[:, hs] = (o * _silu(az_ref[:, hs].astype(F32))).astype(BF16)


def _attention(proj, sink, pos_tab, nblk_tab, layer):
    n = proj.shape[0]
    nb = n // BLK
    prev = lambda b, *_: (jnp.maximum(b - 1, 0),)
    nxt = lambda b, *_: (jnp.minimum(b + 1, nb - 1),)
    kv_specs = [pl.BlockSpec((BLK, ATT_KVW), lambda b, *_, f=f, c=c: f(b) + (c,))
                for c in (CB_AK, CB_AV) for f in (prev, lambda b, *_: (b,), nxt)]
    grid_spec = pltpu.PrefetchScalarGridSpec(
        num_scalar_prefetch=2,
        grid=(nb,),
        in_specs=[
            pl.BlockSpec(memory_space=pltpu.SMEM),
            pl.BlockSpec((BLK, ATT_Q), lambda b, *_: (b, CB_AQ)),
            pl.BlockSpec((BLK, ATT_Q), lambda b, *_: (b, CB_AZ)),
        ] + kv_specs,
        out_specs=pl.BlockSpec((BLK, ATT_Q), lambda b, *_: (b, 0)),
    )
    return pl.pallas_call(
        _attn_body,
        grid_spec=grid_spec,
        out_shape=jax.ShapeDtypeStruct((n, ATT_Q), BF16),
        compiler_params=_cparams("parallel"),
        name="attention",
    )(pos_tab, nblk_tab, sink[layer], proj, proj, proj, proj, proj, proj, proj, proj)


def _l2norm_heads(x):
    parts = []
    for hh in range(DN_H):
        xh = x[:, hh * HD:(hh + 1) * HD]
        parts.append(xh * lax.rsqrt(jnp.sum(xh * xh, axis=-1, keepdims=True) + EPS))
    return parts


def _dn_prep_body(dq, dqp, dqn, dk, dkp, dkn, dv, dvp, dvn, w_ref, valid_ref,
                  q_ref, k_ref, v_ref, buf_ref):
    def conv_silu(cur, prev, nxt, w0):
        x = cur[...].astype(F32)
        x_m1, x_p1 = _shifted(buf_ref, x, prev[...].astype(F32), nxt[...].astype(F32))
        c = (x_m1 * w_ref[0:1, w0:w0 + DN_W] + x * w_ref[1:2, w0:w0 + DN_W]
             + x_p1 * w_ref[2:3, w0:w0 + DN_W])
        return _silu(c)

    q = conv_silu(dq, dqp, dqn, 0)
    for hh, qh in enumerate(_l2norm_heads(q)):
        q_ref[:, hh * HD:(hh + 1) * HD] = qh * (HD ** -0.5)
    k = conv_silu(dk, dkp, dkn, DN_W)
    valid = valid_ref[...]
    for hh, kh in enumerate(_l2norm_heads(k)):
        k_ref[:, hh * HD:(hh + 1) * HD] = kh * valid
    v_ref[...] = conv_silu(dv, dvp, dvn, 2 * DN_W)


def _dn_prep(proj, dn_conv_w, valid, layer):
    n = proj.shape[0]
    out = jax.ShapeDtypeStruct((n, DN_W), F32)
    ospec = pl.BlockSpec((TM, DN_W), lambda i: (i, 0))
    return pl.pallas_call(
        _dn_prep_body,
        grid=(n // TM,),
        in_specs=_halo_specs(DN_W, CB_DQ, n) + _halo_specs(DN_W, CB_DK, n)
        + _halo_specs(DN_W, CB_DV, n) + [
            pl.BlockSpec((None, 3, 3 * DN_W), lambda i: (layer, 0, 0)),
            pl.BlockSpec((TM, 1), lambda i: (i, 0)),
        ],
        out_specs=[ospec, ospec, ospec],
        out_shape=[out, out, out],
        scratch_shapes=[pltpu.VMEM((TM + 2 * HALO, DN_W), F32)],
        compiler_params=_cparams("parallel"),
        name="dn_prep",
    )(proj, proj, proj, proj, proj, proj, proj, proj, proj, dn_conv_w, valid)


def _dn_chunks(chains, xor_idx, s_ref):
    n = len(chains)
    st = []
    for q, k, v, beta, col, row, tot, incl, strict, idx in chains:
        decay = jnp.exp(jnp.where(incl, col - row, -jnp.inf))
        kbeta = k * beta
        gram = _dot_nt(jnp.concatenate([kbeta, q], axis=0).astype(BF16), k.astype(BF16))
        p = -jnp.where(strict, gram[:BLK] * decay, 0.0)
        ecol = jnp.exp(col)
        st.append(dict(
            p=p, qk=(gram[BLK:] * decay).astype(BF16),
            rhs=jnp.concatenate([v * beta, kbeta * ecol], axis=1).astype(BF16),
            qg=(q * ecol).astype(BF16),
            k_tail_t=(k * jnp.exp(tot - col)).T.astype(BF16),
            t=(xor_idx == 0).astype(F32) + jnp.where(xor_idx == 1, p, 0.0)))
    s = 2
    while s < BLK:
        for c in st:
            c["tb"] = c["t"].astype(BF16)
            link = jnp.where((xor_idx >= s) & (xor_idx < 2 * s), c["p"], 0.0).astype(BF16)
            c["lt"] = _dot(link, c["tb"]).astype(BF16)
        for c in st:
            c["t"] = c["t"] + _dot(c["tb"], c["lt"])
        s *= 2
    for c in st:
        c["x"] = _dot(c["t"].astype(BF16), c["rhs"])
    for c, ch in zip(st, chains):
        c["s"] = s_ref[ch[9]]
        c["ws"] = _dot(jnp.concatenate([c["x"][:, HD:].astype(BF16), c["qg"]], axis=0),
                       c["s"].astype(BF16))
    for c in st:
        v_new = c["x"][:, :HD] - c["ws"][:BLK]
        c["r"] = _dot(jnp.concatenate([c["qk"], c["k_tail_t"]], axis=0), v_new.astype(BF16))
    outs = []
    for c, ch in zip(st, chains):
        s_ref[ch[9]] = c["s"] * jnp.exp(ch[6]) + c["r"][BLK:]
        outs.append(c["ws"][BLK:] + c["r"][:BLK])
    return outs


def _dn_scan_body(fblk_ref, bblk_ref, first_ref,
                  qf, kf, vf, gf, gtf, qb, kb, vb, gb, gtb, of_ref, ob_ref, s_ref):
    i = pl.program_id(0)

    @pl.when(first_ref[i] == 1)
    def _():
        s_ref[...] = jnp.zeros_like(s_ref)

    r_i = lax.broadcasted_iota(jnp.int32, (BLK, BLK), 0)
    c_i = lax.broadcasted_iota(jnp.int32, (BLK, BLK), 1)
    lower = (r_i >= c_i).astype(F32)
    upper = (r_i <= c_i).astype(F32)
    xor_idx = r_i ^ c_i
    chains, dests = [], []
    for d, (q_ref, k_ref, v_ref, g_ref, gt_ref, o_ref) in enumerate(
            ((qf, kf, vf, gf, gtf, of_ref), (qb, kb, vb, gb, gtb, ob_ref))):
        if d == 0:
            incl, strict, csum, csum_t, last = r_i >= c_i, r_i > c_i, lower, upper, BLK - 1
        else:
            incl, strict, csum, csum_t, last = r_i <= c_i, r_i < c_i, upper, lower, 0
        gates = g_ref[...]
        gc = jnp.dot(csum, gates, preferred_element_type=F32, precision=HIGHEST)
        gct = jnp.dot(gt_ref[...], csum_t, preferred_element_type=F32, precision=HIGHEST)
        for hh in range(DN_H):
            hs = slice(hh * HD, (hh + 1) * HD)
            cb = d * DN_H + hh
            cg = 2 * DN_H + cb
            chains.append((q_ref[:, hs], k_ref[:, hs], v_ref[:, hs], gates[:, cb:cb + 1],
                           gc[:, cg:cg + 1], gct[cg:cg + 1, :], gc[last:last + 1, cg:cg + 1],
                           incl, strict, cb))
            dests.append((o_ref, hs))
    for (o_ref, hs), o in zip(dests, _dn_chunks(chains, xor_idx, s_ref)):
        o_ref[:, hs] = o


def _dn_scan(q, k, v, gates, gates_t, fblk, bblk, first):
    n = q.shape[0]
    steps = fblk.shape[0]
    fmap = lambda i, f, b, s: (f[i], 0)
    bmap = lambda i, f, b, s: (b[i], 0)
    fmap_t = lambda i, f, b, s: (0, f[i])
    bmap_t = lambda i, f, b, s: (0, b[i])
    wide = lambda m: pl.BlockSpec((BLK, DN_W), m)
    grid_spec = pltpu.PrefetchScalarGridSpec(
        num_scalar_prefetch=3,
        grid=(steps,),
        in_specs=[wide(fmap), wide(fmap), wide(fmap),
                  pl.BlockSpec((BLK, BLK), fmap), pl.BlockSpec((BLK, BLK), fmap_t),
                  wide(bmap), wide(bmap), wide(bmap),
                  pl.BlockSpec((BLK, BLK), bmap), pl.BlockSpec((BLK, BLK), bmap_t)],
        out_specs=[wide(fmap), wide(bmap)],
        scratch_shapes=[pltpu.VMEM((2 * DN_H, HD, HD), F32)],
    )
    out = jax.ShapeDtypeStruct((n, DN_W), F32)
    return pl.pallas_call(
        _dn_scan_body,
        grid_spec=grid_spec,
        out_shape=[out, out],
        compiler_params=_cparams("arbitrary"),
        name="dn_scan",
    )(fblk, bblk, first, q, k, v, gates, gates_t, q, k, v, gates, gates_t)


def _out_proj_body(h_ref, yc_ref, ya_ref, of_ref, ob_ref, dz_ref, nw_ref, valid_ref, w_ref, o_ref):
    parts = []
    for hh in range(DN_H):
        hs = slice(hh * HD, (hh + 1) * HD)
        o = of_ref[:, hs] + ob_ref[:, hs]
        ms = jnp.mean(o * o, axis=-1, keepdims=True)
        on = o * lax.rsqrt(ms + EPS) * nw_ref[...]
        parts.append((on * _silu(dz_ref[:, hs].astype(F32))).astype(BF16))
    y_dn = jnp.concatenate(parts, axis=1)
    c0, c1 = CONV_W, CONV_W + ATT_Q
    acc = (_dot(yc_ref[...], w_ref[0:c0, :]) + _dot(ya_ref[...], w_ref[c0:c1, :])
           + _dot(y_dn, w_ref[c1:, :]))
    o_ref[...] = h_ref[...] + jnp.where(valid_ref[...] > 0.0, acc, 0.0)


def _out_proj(h, y_conv, y_att, o_f, o_b, proj, dn_norm_w, valid, w_out, layer):
    n = h.shape[0]
    row = lambda width: pl.BlockSpec((TM, width), lambda i: (i, 0))
    return pl.pallas_call(
        _out_proj_body,
        grid=(n // TM,),
        in_specs=[row(D_MODEL), row(CONV_W), row(ATT_Q), row(DN_W), row(DN_W),
                  pl.BlockSpec((TM, DN_W), lambda i: (i, CB_DZ)),
                  pl.BlockSpec((None, 1, HD), lambda i: (layer, 0, 0)),
                  row(1),
                  pl.BlockSpec((None, D_MODEL, D_MODEL), lambda i: (layer, 0, 0))],
        out_specs=row(D_MODEL),
        out_shape=jax.ShapeDtypeStruct((n, D_MODEL), F32),
        compiler_params=_cparams("parallel"),
        name="out_proj",
    )(h, y_conv, y_att, o_f, o_b, proj, dn_norm_w, valid, w_out)


def _final_norm_body(h_ref, nw_ref, o_ref):
    x = h_ref[...]
    ms = jnp.mean(x * x, axis=-1, keepdims=True)
    o_ref[...] = x * lax.rsqrt(ms + EPS) * nw_ref[...]


def _final_norm(h, w):
    n = h.shape[0]
    return pl.pallas_call(
        _final_norm_body,
        grid=(n // TM,),
        in_specs=[pl.BlockSpec((TM, D_MODEL), lambda i: (i, 0)),
                  pl.BlockSpec((1, D_MODEL), lambda i: (0, 0))],
        out_specs=pl.BlockSpec((TM, D_MODEL), lambda i: (i, 0)),
        out_shape=jax.ShapeDtypeStruct((n, D_MODEL), F32),
        compiler_params=_cparams("parallel"),
        name="final_norm",
    )(h, w)


def _tables(seq_blocks, n_blocks):
    pos, nblk, fblk, bblk, first = [], [], [], [], []
    start = 0
    for nb in seq_blocks:
        pos += list(range(nb))
        nblk += [nb] * nb
        fblk += [start + j for j in range(nb)]
        bblk += [start + nb - 1 - j for j in range(nb)]
        first += [1] + [0] * (nb - 1)
        start += nb
    pos += [0] * (n_blocks - start)
    nblk += [1] * (n_blocks - start)
    as_i32 = lambda a: jnp.asarray(np.asarray(a, np.int32))
    return as_i32(pos), as_i32(nblk), as_i32(fblk), as_i32(bblk), as_i32(first)


def _forward(xs, meta_tokens, norm_w, w_in, conv_a_w, attn_sink, dn_conv_w, dn_a_log, dn_dt_bias,
             dn_norm_w, w_out, final_norm_w):
    depth = w_in.shape[0]
    lead = jnp.concatenate([jnp.zeros((PAD, D_MODEL), F32), meta_tokens.astype(F32)], axis=0)
    rows, seq_blocks = [], []
    for x in xs:
        for bi in range(x.shape[0]):
            rows += [lead, x[bi]]
            seq_blocks.append((BLK + x.shape[1]) // BLK)
    n_real = sum(seq_blocks) * BLK
    n = -(-n_real // TM_PROJ) * TM_PROJ
    rows.append(jnp.zeros((n - n_real, D_MODEL), F32))
    h = jnp.concatenate(rows, axis=0)

    valid_np = np.zeros((n, 1), np.float32)
    start = 0
    for nb in seq_blocks:
        valid_np[start + PAD:start + nb * BLK] = 1.0
        start += nb * BLK
    valid = jnp.asarray(valid_np)
    pos_tab, nblk_tab, fblk, bblk, first = _tables(seq_blocks, n // BLK)

    c = np.cumsum([0, CONV_W, CONV_W, CONV_W, CONV_W, ATT_Q, ATT_KVW, ATT_KVW, ATT_Q,
                   DN_W, DN_W, DN_W, DN_W, 2 * DN_H, 2 * DN_H])
    w_big = jnp.concatenate([w_in[:, :, c[4]:c[5]], w_in[:, :, c[7]:c[8]], w_in[:, :, c[8]:c[12]],
                             w_in[:, :, c[5]:c[7]], w_in[:, :, c[0]:c[4]]], axis=-1).astype(BF16)
    w_small = jnp.pad(w_in[:, :, c[12]:c[14]], ((0, 0), (0, 0), (0, BLK - N_GATE)))
    w_small_t = jnp.swapaxes(w_small, 1, 2)
    pad_gate = lambda a: jnp.pad(a.reshape(depth, 2 * DN_H), ((0, 0), (2 * DN_H, BLK - N_GATE)))
    a_row = pad_gate(dn_a_log.astype(F32))[:, None, :]
    b_row = pad_gate(dn_dt_bias.astype(F32))[:, None, :]
    a_col, b_col = jnp.swapaxes(a_row, 1, 2), jnp.swapaxes(b_row, 1, 2)
    w_out_b = w_out.astype(BF16)
    norm_w3 = norm_w[:, None, :]
    dn_norm_w3 = dn_norm_w[:, None, :]

    for l in range(depth):
        proj = _in_proj(h, norm_w3, w_big, l)
        gates, gates_t = _gates(h, norm_w3, w_small, w_small_t, a_row, b_row, a_col, b_col, l)
        y_conv = _conv_branch(proj, conv_a_w, l)
        y_att = _attention(proj, attn_sink, pos_tab, nblk_tab, l)
        q, k, v = _dn_prep(proj, dn_conv_w, valid, l)
        o_f, o_b = _dn_scan(q, k, v, gates, gates_t, fblk, bblk, first)
        h = _out_proj(h, y_conv, y_att, o_f, o_b, proj, dn_norm_w3, valid, w_out_b, l)

    out = _final_norm(h, final_norm_w[None, :])
    ys, start = [], 0
    for x in xs:
        b, s = x.shape[0], x.shape[1]
        t = BLK + s
        ys.append(out[start:start + b * t].reshape(b, t, D_MODEL)[:, BLK:])
        start += b * t
    return tuple(ys)


def kernel(x_prompt, x_sample, meta_tokens, norm_w, w_in, conv_a_w, attn_sink, dn_conv_w,
           dn_a_log, dn_dt_bias, dn_norm_w, w_out, final_norm_w):
    return _forward([x_prompt, x_sample], meta_tokens, norm_w, w_in, conv_a_w, attn_sink, dn_conv_w,
                    dn_a_log, dn_dt_bias, dn_norm_w, w_out, final_norm_w)
```

```python
import functools

import numpy as np
import jax
import jax.numpy as jnp
from jax import lax
from jax.experimental import pallas as pl
from jax.experimental.pallas import tpu as pltpu

F32 = jnp.float32
BF16 = jnp.bfloat16
HIGHEST = lax.Precision.HIGHEST

D_MODEL = 2048
N_META = 16
BLK = 128
PAD = BLK - N_META
CONV_W = 512
HD = 128
ATT_HQ = 6
ATT_KV = 2
ATT_G = ATT_HQ // ATT_KV
ATT_Q = ATT_HQ * HD
ATT_KVW = ATT_KV * HD
DN_H = 6
DN_W = DN_H * HD
N_BIG = 4 * CONV_W + 2 * ATT_Q + 2 * ATT_KVW + 4 * DN_W
N_GATE = 4 * DN_H
EPS = 1e-6

CB_AQ, CB_AZ, CB_DQ, CB_DK, CB_DV, CB_DZ = 0, 1, 2, 3, 4, 5
CB_AK, CB_AV = 18, 19
CB_CX, CB_CB, CB_CC, CB_CZ = 10, 11, 12, 13

TM_PROJ = 1024
TN_PROJ = 1024
TM = 512
HALO = 8
VMEM_LIMIT = 56 * 1024 * 1024


def _cparams(*sem):
    return pltpu.CompilerParams(dimension_semantics=sem, vmem_limit_bytes=VMEM_LIMIT)


def _silu(x):
    return x * (1.0 / (1.0 + jnp.exp(-x)))


def _dot(a, b):
    return jnp.dot(a, b, preferred_element_type=F32)


def _dot_nt(a, b, precision=None):
    return lax.dot_general(a, b, (((1,), (1,)), ((), ())), preferred_element_type=F32,
                           precision=precision)


def _softplus(x):
    return jnp.maximum(x, 0.0) + jnp.log1p(jnp.exp(-jnp.abs(x)))


def _gate_fn(x, idx, a_log, bias):
    beta = 1.0 / (1.0 + jnp.exp(-x))
    g = -jnp.exp(a_log) * _softplus(x + bias)
    return jnp.where(idx < 2 * DN_H, beta, jnp.where(idx < N_GATE, g, 0.0))


def _in_proj_body(h_ref, nw_ref, w_ref, ws_ref, wst_ref, arow_ref, brow_ref, acol_ref, bcol_ref,
                  o_ref, g_ref, gt_ref, xn_ref):
    @pl.when(pl.program_id(1) == 0)
    def _():
        rows = 256
        for r in range(0, TM_PROJ, rows):
            x = h_ref[r:r + rows, :]
            ms = jnp.mean(x * x, axis=-1, keepdims=True)
            xn_ref[r:r + rows, :] = (x * lax.rsqrt(ms + EPS) * nw_ref[...]).astype(BF16)
        p = _dot(xn_ref[...], ws_ref[...])
        pt = _dot_nt(wst_ref[...], xn_ref[...])
        g_ref[...] = _gate_fn(p, lax.broadcasted_iota(jnp.int32, p.shape, 1), arow_ref[...], brow_ref[...])
        gt_ref[...] = _gate_fn(pt, lax.broadcasted_iota(jnp.int32, pt.shape, 0), acol_ref[...], bcol_ref[...])

    o_ref[...] = _dot(xn_ref[...], w_ref[...]).astype(BF16)


def _in_proj(h, norm_w, w_big, w_small, w_small_t, arow, brow, acol, bcol, layer):
    n = h.shape[0]
    lay3 = lambda i, j: (layer, 0, 0)
    return pl.pallas_call(
        _in_proj_body,
        grid=(n // TM_PROJ, N_BIG // TN_PROJ),
        in_specs=[
            pl.BlockSpec((TM_PROJ, D_MODEL), lambda i, j: (i, 0)),
            pl.BlockSpec((None, 1, D_MODEL), lay3),
            pl.BlockSpec((None, D_MODEL, TN_PROJ), lambda i, j: (layer, 0, j)),
            pl.BlockSpec((None, D_MODEL, BLK), lay3),
            pl.BlockSpec((None, BLK, D_MODEL), lay3),
            pl.BlockSpec((None, 1, BLK), lay3),
            pl.BlockSpec((None, 1, BLK), lay3),
            pl.BlockSpec((None, BLK, 1), lay3),
            pl.BlockSpec((None, BLK, 1), lay3),
        ],
        out_specs=[pl.BlockSpec((TM_PROJ, TN_PROJ), lambda i, j: (i, j)),
                   pl.BlockSpec((TM_PROJ, BLK), lambda i, j: (i, 0)),
                   pl.BlockSpec((BLK, TM_PROJ), lambda i, j: (0, i))],
        out_shape=[jax.ShapeDtypeStruct((n, N_BIG), BF16),
                   jax.ShapeDtypeStruct((n, BLK), F32), jax.ShapeDtypeStruct((BLK, n), F32)],
        scratch_shapes=[pltpu.VMEM((TM_PROJ, D_MODEL), BF16)],
        compiler_params=_cparams("parallel", "arbitrary"),
        name="in_proj",
    )(h, norm_w, w_big, w_small, w_small_t, arow, brow, acol, bcol)


def _halo_specs(width, col_block, n):
    last = n // HALO - 1
    per = TM // HALO
    return [
        pl.BlockSpec((TM, width), lambda i: (i, col_block)),
        pl.BlockSpec((HALO, width), lambda i: (jnp.maximum(i * per - 1, 0), col_block)),
        pl.BlockSpec((HALO, width), lambda i: (jnp.minimum((i + 1) * per, last), col_block)),
    ]


def _shifted(buf_ref, cur, prev, nxt):
    i = pl.program_id(0)
    has_prev = (i > 0).astype(F32)
    has_next = (i < pl.num_programs(0) - 1).astype(F32)
    buf_ref[0:HALO, :] = prev * has_prev
    buf_ref[HALO:HALO + TM, :] = cur
    buf_ref[HALO + TM:2 * HALO + TM, :] = nxt * has_next
    return buf_ref[HALO - 1:HALO - 1 + TM, :], buf_ref[HALO + 1:HALO + 1 + TM, :]


def _conv_body(cx, cxp, cxn, cc, ccp, ccn, cb, cz, w_ref, o_ref, buf_ref):
    u = cc[...].astype(F32) * cx[...].astype(F32)
    up = ccp[...].astype(F32) * cxp[...].astype(F32)
    un = ccn[...].astype(F32) * cxn[...].astype(F32)
    u_m1, u_p1 = _shifted(buf_ref, u, up, un)
    conv = u_m1 * w_ref[0:1, :] + u * w_ref[1:2, :] + u_p1 * w_ref[2:3, :]
    o_ref[...] = (cb[...].astype(F32) * conv * _silu(cz[...].astype(F32))).astype(BF16)


def _conv_branch(proj, conv_w, layer):
    n = proj.shape[0]
    return pl.pallas_call(
        _conv_body,
        grid=(n // TM,),
        in_specs=_halo_specs(CONV_W, CB_CX, n) + _halo_specs(CONV_W, CB_CC, n) + [
            pl.BlockSpec((TM, CONV_W), lambda i: (i, CB_CB)),
            pl.BlockSpec((TM, CONV_W), lambda i: (i, CB_CZ)),
            pl.BlockSpec((None, 3, CONV_W), lambda i: (layer, 0, 0)),
        ],
        out_specs=pl.BlockSpec((TM, CONV_W), lambda i: (i, 0)),
        out_shape=jax.ShapeDtypeStruct((n, CONV_W), BF16),
        scratch_shapes=[pltpu.VMEM((TM + 2 * HALO, CONV_W), F32)],
        compiler_params=_cparams("parallel"),
        name="conv_branch",
    )(proj, proj, proj, proj, proj, proj, proj, proj, conv_w)


def _attn_body(pos_ref, nblk_ref, sink_ref, q_ref, az_ref, kp, kc, kn, vp, vc, vn, o_ref):
    b = pl.program_id(0)
    qi = lax.broadcasted_iota(jnp.int32, (BLK, 3 * BLK), 0)
    kj = lax.broadcasted_iota(jnp.int32, (BLK, 3 * BLK), 1) - BLK
    dist = jnp.abs(qi - kj)
    kabs = pos_ref[b] * BLK + kj
    allowed = (dist <= BLK) & (kabs >= PAD) & (kabs < nblk_ref[b] * BLK)
    distf = dist.astype(F32)
    scores = []
    for kvh in range(ATT_KV):
        ks = slice(kvh * HD, (kvh + 1) * HD)
        k3 = jnp.concatenate([kp[:, ks], kc[:, ks], kn[:, ks]], axis=0)
        q3 = jnp.concatenate([q_ref[:, (kvh * ATT_G + g) * HD:(kvh * ATT_G + g + 1) * HD]
                              for g in range(ATT_G)], axis=0)
        scores.append(_dot_nt(q3, k3))
    probs, dens = [], []
    for head in range(ATT_HQ):
        kvh, g = divmod(head, ATT_G)
        slope = float(2.0 ** (-8.0 * (head + 1) / ATT_HQ))
        s = scores[kvh][g * BLK:(g + 1) * BLK] * (HD ** -0.5) - slope * distf
        s = jnp.where(allowed, s, -jnp.inf)
        sk = sink_ref[head]
        m = jnp.maximum(jnp.max(s, axis=-1, keepdims=True), sk)
        p = jnp.exp(s - m)
        dens.append(jnp.sum(p, axis=-1, keepdims=True) + jnp.exp(sk - m))
        probs.append(p.astype(BF16))
    for kvh in range(ATT_KV):
        ks = slice(kvh * HD, (kvh + 1) * HD)
        v3 = jnp.concatenate([vp[:, ks], vc[:, ks], vn[:, ks]], axis=0)
        o3 = _dot(jnp.concatenate(probs[kvh * ATT_G:(kvh + 1) * ATT_G], axis=0), v3)
        for g in range(ATT_G):
            head = kvh * ATT_G + g
            hs = slice(head * HD, (head + 1) * HD)
            o = o3[g * BLK:(g + 1) * BLK] / dens[head]
            o_ref[:, hs] = (o * _silu(az_ref[:, hs].astype(F32))).astype(BF16)


def _attention(proj, sink, pos_tab, nblk_tab, layer):
    n = proj.shape[0]
    nb = n // BLK
    prev = lambda b, *_: (jnp.maximum(b - 1, 0),)
    nxt = lambda b, *_: (jnp.minimum(b + 1, nb - 1),)
    kv_specs = [pl.BlockSpec((BLK, ATT_KVW), lambda b, *_, f=f, c=c: f(b) + (c,))
                for c in (CB_AK, CB_AV) for f in (prev, lambda b, *_: (b,), nxt)]
    grid_spec = pltpu.PrefetchScalarGridSpec(
        num_scalar_prefetch=2,
        grid=(nb,),
        in_specs=[
            pl.BlockSpec(memory_space=pltpu.SMEM),
            pl.BlockSpec((BLK, ATT_Q), lambda b, *_: (b, CB_AQ)),
            pl.BlockSpec((BLK, ATT_Q), lambda b, *_: (b, CB_AZ)),
        ] + kv_specs,
        out_specs=pl.BlockSpec((BLK, ATT_Q), lambda b, *_: (b, 0)),
    )
    return pl.pallas_call(
        _attn_body,
        grid_spec=grid_spec,
        out_shape=jax.ShapeDtypeStruct((n, ATT_Q), BF16),
        compiler_params=_cparams("parallel"),
        name="attention",
    )(pos_tab, nblk_tab, sink[layer], proj, proj, proj, proj, proj, proj, proj, proj)


def _l2norm_heads(x):
    parts = []
    for hh in range(DN_H):
        xh = x[:, hh * HD:(hh + 1) * HD]
        parts.append(xh * lax.rsqrt(jnp.sum(xh * xh, axis=-1, keepdims=True) + EPS))
    return parts


def _dn_prep_body(dq, dqp, dqn, dk, dkp, dkn, dv, dvp, dvn, w_ref, valid_ref,
                  q_ref, k_ref, v_ref, buf_ref):
    def conv_silu(cur, prev, nxt, w0):
        x = cur[...].astype(F32)
        x_m1, x_p1 = _shifted(buf_ref, x, prev[...].astype(F32), nxt[...].astype(F32))
        c = (x_m1 * w_ref[0:1, w0:w0 + DN_W] + x * w_ref[1:2, w0:w0 + DN_W]
             + x_p1 * w_ref[2:3, w0:w0 + DN_W])
        return _silu(c)

    q = conv_silu(dq, dqp, dqn, 0)
    for hh, qh in enumerate(_l2norm_heads(q)):
        q_ref[:, hh * HD:(hh + 1) * HD] = qh * (HD ** -0.5)
    k = conv_silu(dk, dkp, dkn, DN_W)
    valid = valid_ref[...]
    for hh, kh in enumerate(_l2norm_heads(k)):
        k_ref[:, hh * HD:(hh + 1) * HD] = kh * valid
    v_ref[...] = conv_silu(dv, dvp, dvn, 2 * DN_W)


def _dn_prep(proj, dn_conv_w, valid, layer):
    n = proj.shape[0]
    out = jax.ShapeDtypeStruct((n, DN_W), F32)
    ospec = pl.BlockSpec((TM, DN_W), lambda i: (i, 0))
    return pl.pallas_call(
        _dn_prep_body,
        grid=(n // TM,),
        in_specs=_halo_specs(DN_W, CB_DQ, n) + _halo_specs(DN_W, CB_DK, n)
        + _halo_specs(DN_W, CB_DV, n) + [
            pl.BlockSpec((None, 3, 3 * DN_W), lambda i: (layer, 0, 0)),
            pl.BlockSpec((TM, 1), lambda i: (i, 0)),
        ],
        out_specs=[ospec, ospec, ospec],
        out_shape=[out, out, out],
        scratch_shapes=[pltpu.VMEM((TM + 2 * HALO, DN_W), F32)],
        compiler_params=_cparams("parallel"),
        name="dn_prep",
    )(proj, proj, proj, proj, proj, proj, proj, proj, proj, dn_conv_w, valid)


def _dn_chunks(chains, xor_idx, s_ref):
    n = len(chains)
    st = []
    for q, k, v, beta, col, row, tot, incl, strict, idx in chains:
        decay = jnp.exp(jnp.where(incl, col - row, -jnp.inf))
        kbeta = k * beta
        gram = _dot_nt(jnp.concatenate([kbeta, q], axis=0).astype(BF16), k.astype(BF16))
        p = -jnp.where(strict, gram[:BLK] * decay, 0.0)
        ecol = jnp.exp(col)
        st.append(dict(
            p=p, qk=(gram[BLK:] * decay).astype(BF16),
            rhs=jnp.concatenate([v * beta, kbeta * ecol], axis=1).astype(BF16),
            qg=(q * ecol).astype(BF16),
            k_tail_t=(k * jnp.exp(tot - col)).T.astype(BF16),
            t=(xor_idx == 0).astype(F32) + jnp.where(xor_idx == 1, p, 0.0)))
    s = 2
    while s < BLK:
        for c in st:
            c["tb"] = c["t"].astype(BF16)
            link = jnp.where((xor_idx >= s) & (xor_idx < 2 * s), c["p"], 0.0).astype(BF16)
            c["lt"] = _dot(link, c["tb"]).astype(BF16)
        for c in st:
            c["t"] = c["t"] + _dot(c["tb"], c["lt"])
        s *= 2
    for c in st:
        c["x"] = _dot(c["t"].astype(BF16), c["rhs"])
    for c, ch in zip(st, chains):
        c["s"] = s_ref[ch[9]]
        c["ws"] = _dot(jnp.concatenate([c["x"][:, HD:].astype(BF16), c["qg"]], axis=0),
                       c["s"].astype(BF16))
    for c in st:
        v_new = c["x"][:, :HD] - c["ws"][:BLK]
        c["r"] = _dot(jnp.concatenate([c["qk"], c["k_tail_t"]], axis=0), v_new.astype(BF16))
    outs = []
    for c, ch in zip(st, chains):
        s_ref[ch[9]] = c["s"] * jnp.exp(ch[6]) + c["r"][BLK:]
        outs.append(c["ws"][BLK:] + c["r"][:BLK])
    return outs


def _dn_scan_body(fblk_ref, bblk_ref, first_ref,
                  qf, kf, vf, gf, gtf, qb, kb, vb, gb, gtb, of_ref, ob_ref, s_ref):
    i = pl.program_id(0)

    @pl.when(first_ref[i] == 1)
    def _():
        s_ref[...] = jnp.zeros_like(s_ref)

    r_i = lax.broadcasted_iota(jnp.int32, (BLK, BLK), 0)
    c_i = lax.broadcasted_iota(jnp.int32, (BLK, BLK), 1)
    lower = (r_i >= c_i).astype(F32)
    upper = (r_i <= c_i).astype(F32)
    xor_idx = r_i ^ c_i
    chains, dests = [], []
    for d, (q_ref, k_ref, v_ref, g_ref, gt_ref, o_ref) in enumerate(
            ((qf, kf, vf, gf, gtf, of_ref), (qb, kb, vb, gb, gtb, ob_ref))):
        if d == 0:
            incl, strict, csum, csum_t, last = r_i >= c_i, r_i > c_i, lower, upper, BLK - 1
        else:
            incl, strict, csum, csum_t, last = r_i <= c_i, r_i < c_i, upper, lower, 0
        gates = g_ref[...]
        gc = jnp.dot(csum, gates, preferred_element_type=F32, precision=HIGHEST)
        gct = jnp.dot(gt_ref[...], csum_t, preferred_element_type=F32, precision=HIGHEST)
        for hh in range(DN_H):
            hs = slice(hh * HD, (hh + 1) * HD)
            cb = d * DN_H + hh
            cg = 2 * DN_H + cb
            chains.append((q_ref[:, hs], k_ref[:, hs], v_ref[:, hs], gates[:, cb:cb + 1],
                           gc[:, cg:cg + 1], gct[cg:cg + 1, :], gc[last:last + 1, cg:cg + 1],
                           incl, strict, cb))
            dests.append((o_ref, hs))
    for (o_ref, hs), o in zip(dests, _dn_chunks(chains, xor_idx, s_ref)):
        o_ref[:, hs] = o


def _dn_scan(q, k, v, gates, gates_t, fblk, bblk, first):
    n = q.shape[0]
    steps = fblk.shape[0]
    fmap = lambda i, f, b, s: (f[i], 0)
    bmap = lambda i, f, b, s: (b[i], 0)
    fmap_t = lambda i, f, b, s: (0, f[i])
    bmap_t = lambda i, f, b, s: (0, b[i])
    wide = lambda m: pl.BlockSpec((BLK, DN_W), m)
    grid_spec = pltpu.PrefetchScalarGridSpec(
        num_scalar_prefetch=3,
        grid=(steps,),
        in_specs=[wide(fmap), wide(fmap), wide(fmap),
                  pl.BlockSpec((BLK, BLK), fmap), pl.BlockSpec((BLK, BLK), fmap_t),
                  wide(bmap), wide(bmap), wide(bmap),
                  pl.BlockSpec((BLK, BLK), bmap), pl.BlockSpec((BLK, BLK), bmap_t)],
        out_specs=[wide(fmap), wide(bmap)],
        scratch_shapes=[pltpu.VMEM((2 * DN_H, HD, HD), F32)],
    )
    out = jax.ShapeDtypeStruct((n, DN_W), F32)
    return pl.pallas_call(
        _dn_scan_body,
        grid_spec=grid_spec,
        out_shape=[out, out],
        compiler_params=_cparams("arbitrary"),
        name="dn_scan",
    )(fblk, bblk, first, q, k, v, gates, gates_t, q, k, v, gates, gates_t)


def _out_proj_body(h_ref, yc_ref, ya_ref, of_ref, ob_ref, dz_ref, nw_ref, valid_ref, w_ref, o_ref):
    parts = []
    for hh in range(DN_H):
        hs = slice(hh * HD, (hh + 1) * HD)
        o = of_ref[:, hs] + ob_ref[:, hs]
        ms = jnp.mean(o * o, axis=-1, keepdims=True)
        on = o * lax.rsqrt(ms + EPS) * nw_ref[...]
        parts.append((on * _silu(dz_ref[:, hs].astype(F32))).astype(BF16))
    y_dn = jnp.concatenate(parts, axis=1)
    c0, c1 = CONV_W, CONV_W + ATT_Q
    acc = (_dot(yc_ref[...], w_ref[0:c0, :]) + _dot(ya_ref[...], w_ref[c0:c1, :])
           + _dot(y_dn, w_ref[c1:, :]))
    o_ref[...] = h_ref[...] + jnp.where(valid_ref[...] > 0.0, acc, 0.0)


def _out_proj(h, y_conv, y_att, o_f, o_b, proj, dn_norm_w, valid, w_out, layer):
    n = h.shape[0]
    row = lambda width: pl.BlockSpec((TM, width), lambda i: (i, 0))
    return pl.pallas_call(
        _out_proj_body,
        grid=(n // TM,),
        in_specs=[row(D_MODEL), row(CONV_W), row(ATT_Q), row(DN_W), row(DN_W),
                  pl.BlockSpec((TM, DN_W), lambda i: (i, CB_DZ)),
                  pl.BlockSpec((None, 1, HD), lambda i: (layer, 0, 0)),
                  row(1),
                  pl.BlockSpec((None, D_MODEL, D_MODEL), lambda i: (layer, 0, 0))],
        out_specs=row(D_MODEL),
        out_shape=jax.ShapeDtypeStruct((n, D_MODEL), F32),
        compiler_params=_cparams("parallel"),
        name="out_proj",
    )(h, y_conv, y_att, o_f, o_b, proj, dn_norm_w, valid, w_out)


def _final_norm_body(h_ref, nw_ref, o_ref):
    x = h_ref[...]
    ms = jnp.mean(x * x, axis=-1, keepdims=True)
    o_ref[...] = x * lax.rsqrt(ms + EPS) * nw_ref[...]


def _final_norm(h, w, start_block, batch, seq):
    nb = (BLK + seq) // BLK
    return pl.pallas_call(
        _final_norm_body,
        grid=(batch, seq // BLK),
        in_specs=[pl.BlockSpec((BLK, D_MODEL), lambda b, j: (start_block + b * nb + 1 + j, 0)),
                  pl.BlockSpec((1, D_MODEL), lambda b, j: (0, 0))],
        out_specs=pl.BlockSpec((None, BLK, D_MODEL), lambda b, j: (b, j, 0)),
        out_shape=jax.ShapeDtypeStruct((batch, seq, D_MODEL), F32),
        compiler_params=_cparams("parallel", "parallel"),
        name="final_norm",
    )(h, w)


W_COLS = 256


def _regroup_body(perm_ref, w_ref, o_ref):
    o_ref[...] = w_ref[...].astype(BF16)


def _regroup_w_in(w_in, perm):
    depth = w_in.shape[0]
    grid_spec = pltpu.PrefetchScalarGridSpec(
        num_scalar_prefetch=1,
        grid=(depth, N_BIG // W_COLS),
        in_specs=[pl.BlockSpec((None, D_MODEL, W_COLS), lambda l, j, perm: (l, 0, perm[j]))],
        out_specs=pl.BlockSpec((None, D_MODEL, W_COLS), lambda l, j, perm: (l, 0, j)),
    )
    return pl.pallas_call(
        _regroup_body,
        grid_spec=grid_spec,
        out_shape=jax.ShapeDtypeStruct((depth, D_MODEL, N_BIG), BF16),
        compiler_params=_cparams("parallel", "parallel"),
        name="regroup_w_in",
    )(perm, w_in)


def _tables(seq_blocks, n_blocks):
    pos, nblk, fblk, bblk, first = [], [], [], [], []
    start = 0
    for nb in seq_blocks:
        pos += list(range(nb))
        nblk += [nb] * nb
        fblk += [start + j for j in range(nb)]
        bblk += [start + nb - 1 - j for j in range(nb)]
        first += [1] + [0] * (nb - 1)
        start += nb
    pos += [0] * (n_blocks - start)
    nblk += [1] * (n_blocks - start)
    as_i32 = lambda a: jnp.asarray(np.asarray(a, np.int32))
    return as_i32(pos), as_i32(nblk), as_i32(fblk), as_i32(bblk), as_i32(first)


def _forward(xs, meta_tokens, norm_w, w_in, conv_a_w, attn_sink, dn_conv_w, dn_a_log, dn_dt_bias,
             dn_norm_w, w_out, final_norm_w):
    depth = w_in.shape[0]
    lead = jnp.concatenate([jnp.zeros((PAD, D_MODEL), F32), meta_tokens.astype(F32)], axis=0)
    rows, seq_blocks = [], []
    for x in xs:
        for bi in range(x.shape[0]):
            rows += [lead, x[bi]]
            seq_blocks.append((BLK + x.shape[1]) // BLK)
    n_real = sum(seq_blocks) * BLK
    n = -(-n_real // TM_PROJ) * TM_PROJ
    rows.append(jnp.zeros((n - n_real, D_MODEL), F32))
    h = jnp.concatenate(rows, axis=0)

    valid_np = np.zeros((n, 1), np.float32)
    start = 0
    for nb in seq_blocks:
        valid_np[start + PAD:start + nb * BLK] = 1.0
        start += nb * BLK
    valid = jnp.asarray(valid_np)
    pos_tab, nblk_tab, fblk, bblk, first = _tables(seq_blocks, n // BLK)

    c = np.cumsum([0, CONV_W, CONV_W, CONV_W, CONV_W, ATT_Q, ATT_KVW, ATT_KVW, ATT_Q,
                   DN_W, DN_W, DN_W, DN_W, 2 * DN_H, 2 * DN_H])
    order = [(c[4], c[5]), (c[7], c[8]), (c[8], c[12]), (c[5], c[7]), (c[0], c[4])]
    perm = np.concatenate([np.arange(a // W_COLS, b // W_COLS) for a, b in order]).astype(np.int32)
    w_big = _regroup_w_in(w_in, jnp.asarray(perm))
    w_small = jnp.pad(w_in[:, :, c[12]:c[14]], ((0, 0), (0, 0), (0, BLK - N_GATE))).astype(BF16)
    w_small_t = jnp.swapaxes(w_small, 1, 2)
    pad_gate = lambda a: jnp.pad(a.reshape(depth, 2 * DN_H), ((0, 0), (2 * DN_H, BLK - N_GATE)))
    a_row = pad_gate(dn_a_log.astype(F32))[:, None, :]
    b_row = pad_gate(dn_dt_bias.astype(F32))[:, None, :]
    a_col, b_col = jnp.swapaxes(a_row, 1, 2), jnp.swapaxes(b_row, 1, 2)
    w_out_b = w_out.astype(BF16)
    norm_w3 = norm_w[:, None, :]
    dn_norm_w3 = dn_norm_w[:, None, :]

    for l in range(depth):
        proj, gates, gates_t = _in_proj(h, norm_w3, w_big, w_small, w_small_t,
                                        a_row, b_row, a_col, b_col, l)
        y_conv = _conv_branch(proj, conv_a_w, l)
        y_att = _attention(proj, attn_sink, pos_tab, nblk_tab, l)
        q, k, v = _dn_prep(proj, dn_conv_w, valid, l)
        o_f, o_b = _dn_scan(q, k, v, gates, gates_t, fblk, bblk, first)
        h = _out_proj(h, y_conv, y_att, o_f, o_b, proj, dn_norm_w3, valid, w_out_b, l)

    ys, start = [], 0
    for x in xs:
        b, s = x.shape[0], x.shape[1]
        ys.append(_final_norm(h, final_norm_w[None, :], start, b, s))
        start += b * (BLK + s) // BLK
    return tuple(ys)


def kernel(x_prompt, x_sample, meta_tokens, norm_w, w_in, conv_a_w, attn_sink, dn_conv_w,
           dn_a_log, dn_dt_bias, dn_norm_w, w_out, final_norm_w):
    return _forward([x_prompt, x_sample], meta_tokens, norm_w, w_in, conv_a_w, attn_sink, dn_conv_w,
                    dn_a_log, dn_dt_bias, dn_norm_w, w_out, final_norm_w)
```

```python
import numpy as np
import jax
import jax.numpy as jnp
from jax import lax
from jax.experimental import pallas as pl
from jax.experimental.pallas import tpu as pltpu

F32 = jnp.float32
BF16 = jnp.bfloat16
HIGHEST = lax.Precision.HIGHEST

D_MODEL = 2048
N_META = 16
BLK = 128
PAD = BLK - N_META
CONV_W = 512
HD = 128
ATT_HQ = 6
ATT_KV = 2
ATT_G = ATT_HQ // ATT_KV
ATT_Q = ATT_HQ * HD
ATT_KVW = ATT_KV * HD
DN_H = 6
DN_W = DN_H * HD
N_BIG = 4 * CONV_W + 2 * ATT_Q + 2 * ATT_KVW + 4 * DN_W
N_GATE = 4 * DN_H
EPS = 1e-6

CB_AQ, CB_AZ, CB_DQ, CB_DK, CB_DV, CB_DZ = 0, 1, 2, 3, 4, 5
CB_AK, CB_AV = 18, 19
CB_CX, CB_CB, CB_CC, CB_CZ = 10, 11, 12, 13

TM_PROJ = 1024
TN_PROJ = 1024
TM = 512
HALO = 8
VMEM_LIMIT = 56 * 1024 * 1024


def _cparams(*sem):
    return pltpu.CompilerParams(dimension_semantics=sem, vmem_limit_bytes=VMEM_LIMIT)


def _silu(x):
    return x * (1.0 / (1.0 + jnp.exp(-x)))


def _dot(a, b):
    return jnp.dot(a, b, preferred_element_type=F32)


def _dot_nt(a, b, precision=None):
    return lax.dot_general(a, b, (((1,), (1,)), ((), ())), preferred_element_type=F32,
                           precision=precision)


def _softplus(x):
    return jnp.maximum(x, 0.0) + jnp.log1p(jnp.exp(-jnp.abs(x)))


def _gate_fn(x, idx, a_log, bias):
    beta = 1.0 / (1.0 + jnp.exp(-x))
    g = -jnp.exp(a_log) * _softplus(x + bias)
    return jnp.where(idx < 2 * DN_H, beta, jnp.where(idx < N_GATE, g, 0.0))


def _in_proj_body(h_ref, nw_ref, w_ref, ws_ref, wst_ref, arow_ref, brow_ref, acol_ref, bcol_ref,
                  o_ref, g_ref, gt_ref, xn_ref):
    @pl.when(pl.program_id(1) == 0)
    def _():
        rows = 256
        for r in range(0, TM_PROJ, rows):
            x = h_ref[r:r + rows, :]
            ms = jnp.mean(x * x, axis=-1, keepdims=True)
            xn_ref[r:r + rows, :] = (x * lax.rsqrt(ms + EPS) * nw_ref[...]).astype(BF16)
        p = _dot(xn_ref[...], ws_ref[...])
        pt = _dot_nt(wst_ref[...], xn_ref[...])
        g_ref[...] = _gate_fn(p, lax.broadcasted_iota(jnp.int32, p.shape, 1), arow_ref[...], brow_ref[...])
        gt_ref[...] = _gate_fn(pt, lax.broadcasted_iota(jnp.int32, pt.shape, 0), acol_ref[...], bcol_ref[...])

    o_ref[...] = _dot(xn_ref[...], w_ref[...]).astype(BF16)


def _in_proj(h, norm_w, w_big, w_small, w_small_t, arow, brow, acol, bcol, layer):
    n = h.shape[0]
    lay3 = lambda i, j: (layer, 0, 0)
    return pl.pallas_call(
        _in_proj_body,
        grid=(n // TM_PROJ, N_BIG // TN_PROJ),
        in_specs=[
            pl.BlockSpec((TM_PROJ, D_MODEL), lambda i, j: (i, 0)),
            pl.BlockSpec((None, 1, D_MODEL), lay3),
            pl.BlockSpec((None, D_MODEL, TN_PROJ), lambda i, j: (layer, 0, j)),
            pl.BlockSpec((None, D_MODEL, BLK), lay3),
            pl.BlockSpec((None, BLK, D_MODEL), lay3),
            pl.BlockSpec((None, 1, BLK), lay3),
            pl.BlockSpec((None, 1, BLK), lay3),
            pl.BlockSpec((None, BLK, 1), lay3),
            pl.BlockSpec((None, BLK, 1), lay3),
        ],
        out_specs=[pl.BlockSpec((TM_PROJ, TN_PROJ), lambda i, j: (i, j)),
                   pl.BlockSpec((TM_PROJ, BLK), lambda i, j: (i, 0)),
                   pl.BlockSpec((BLK, TM_PROJ), lambda i, j: (0, i))],
        out_shape=[jax.ShapeDtypeStruct((n, N_BIG), BF16),
                   jax.ShapeDtypeStruct((n, BLK), F32), jax.ShapeDtypeStruct((BLK, n), F32)],
        scratch_shapes=[pltpu.VMEM((TM_PROJ, D_MODEL), BF16)],
        compiler_params=_cparams("parallel", "arbitrary"),
        name="in_proj",
    )(h, norm_w, w_big, w_small, w_small_t, arow, brow, acol, bcol)


def _halo_specs(width, col_block, n):
    last = n // HALO - 1
    per = TM // HALO
    return [
        pl.BlockSpec((TM, width), lambda i: (i, col_block)),
        pl.BlockSpec((HALO, width), lambda i: (jnp.maximum(i * per - 1, 0), col_block)),
        pl.BlockSpec((HALO, width), lambda i: (jnp.minimum((i + 1) * per, last), col_block)),
    ]


def _shifted(buf_ref, cur, prev, nxt):
    i = pl.program_id(0)
    has_prev = (i > 0).astype(F32)
    has_next = (i < pl.num_programs(0) - 1).astype(F32)
    buf_ref[0:HALO, :] = prev * has_prev
    buf_ref[HALO:HALO + TM, :] = cur
    buf_ref[HALO + TM:2 * HALO + TM, :] = nxt * has_next
    return buf_ref[HALO - 1:HALO - 1 + TM, :], buf_ref[HALO + 1:HALO + 1 + TM, :]


def _conv_body(cx, cxp, cxn, cc, ccp, ccn, cb, cz, w_ref, o_ref, buf_ref):
    u = cc[...].astype(F32) * cx[...].astype(F32)
    up = ccp[...].astype(F32) * cxp[...].astype(F32)
    un = ccn[...].astype(F32) * cxn[...].astype(F32)
    u_m1, u_p1 = _shifted(buf_ref, u, up, un)
    conv = u_m1 * w_ref[0:1, :] + u * w_ref[1:2, :] + u_p1 * w_ref[2:3, :]
    o_ref[...] = (cb[...].astype(F32) * conv * _silu(cz[...].astype(F32))).astype(BF16)


def _conv_branch(proj, conv_w, layer):
    n = proj.shape[0]
    return pl.pallas_call(
        _conv_body,
        grid=(n // TM,),
        in_specs=_halo_specs(CONV_W, CB_CX, n) + _halo_specs(CONV_W, CB_CC, n) + [
            pl.BlockSpec((TM, CONV_W), lambda i: (i, CB_CB)),
            pl.BlockSpec((TM, CONV_W), lambda i: (i, CB_CZ)),
            pl.BlockSpec((None, 3, CONV_W), lambda i: (layer, 0, 0)),
        ],
        out_specs=pl.BlockSpec((TM, CONV_W), lambda i: (i, 0)),
        out_shape=jax.ShapeDtypeStruct((n, CONV_W), BF16),
        scratch_shapes=[pltpu.VMEM((TM + 2 * HALO, CONV_W), F32)],
        compiler_params=_cparams("parallel"),
        name="conv_branch",
    )(proj, proj, proj, proj, proj, proj, proj, proj, conv_w)


ATT_QB = 2


def _attn_body(pos_ref, nblk_ref, sink_ref, q_ref, az_ref, kp, kc, kn, vp, vc, vn, o_ref):
    i = pl.program_id(0)
    qi = lax.broadcasted_iota(jnp.int32, (BLK, 3 * BLK), 0)
    kj = lax.broadcasted_iota(jnp.int32, (BLK, 3 * BLK), 1) - BLK
    dist = jnp.abs(qi - kj)
    distf = dist.astype(F32)
    allowed = []
    for j in range(ATT_QB):
        b = i * ATT_QB + j
        kabs = pos_ref[b] * BLK + kj
        allowed.append((dist <= BLK) & (kabs >= PAD) & (kabs < nblk_ref[b] * BLK))

    def band(prev_ref, cur_ref, next_ref, j, cols):
        blocks = ([prev_ref[:, cols]] + [cur_ref[r * BLK:(r + 1) * BLK, cols] for r in range(ATT_QB)]
                  + [next_ref[:, cols]])
        return jnp.concatenate(blocks[j:j + 3], axis=0)

    scores = {}
    for j in range(ATT_QB):
        rows = slice(j * BLK, (j + 1) * BLK)
        for kvh in range(ATT_KV):
            k3 = band(kp, kc, kn, j, slice(kvh * HD, (kvh + 1) * HD))
            q3 = jnp.concatenate([q_ref[rows, (kvh * ATT_G + g) * HD:(kvh * ATT_G + g + 1) * HD]
                                  for g in range(ATT_G)], axis=0)
            scores[j, kvh] = _dot_nt(q3, k3)
    probs, dens = {}, {}
    for j in range(ATT_QB):
        for head in range(ATT_HQ):
            kvh, g = divmod(head, ATT_G)
            slope = float(2.0 ** (-8.0 * (head + 1) / ATT_HQ))
            s = scores[j, kvh][g * BLK:(g + 1) * BLK] * (HD ** -0.5) - slope * distf
            s = jnp.where(allowed[j], s, -jnp.inf)
            sk = sink_ref[head]
            m = jnp.maximum(jnp.max(s, axis=-1, keepdims=True), sk)
            p = jnp.exp(s - m)
            dens[j, head] = jnp.sum(p, axis=-1, keepdims=True) + jnp.exp(sk - m)
            probs[j, head] = p.astype(BF16)
    for j in range(ATT_QB):
        rows = slice(j * BLK, (j + 1) * BLK)
        for kvh in range(ATT_KV):
            v3 = band(vp, vc, vn, j, slice(kvh * HD, (kvh + 1) * HD))
            p3 = jnp.concatenate([probs[j, kvh * ATT_G + g] for g in range(ATT_G)], axis=0)
            o3 = _dot(p3, v3)
            for g in range(ATT_G):
                head = kvh * ATT_G + g
                hs = slice(head * HD, (head + 1) * HD)
                o = o3[g * BLK:(g + 1) * BLK] / dens[j, head]
                o_ref[rows, hs] = (o * _silu(az_ref[rows, hs].astype(F32))).astype(BF16)


def _attention(proj, sink, pos_tab, nblk_tab, layer):
    n = proj.shape[0]
    nb = n // BLK
    tq = ATT_QB * BLK
    prev = lambda i, *_: (jnp.maximum(i * ATT_QB - 1, 0),)
    nxt = lambda i, *_: (jnp.minimum((i + 1) * ATT_QB, nb - 1),)
    kv_specs = []
    for c in (CB_AK, CB_AV):
        kv_specs += [pl.BlockSpec((BLK, ATT_KVW), lambda i, *_, c=c: prev(i) + (c,)),
                     pl.BlockSpec((tq, ATT_KVW), lambda i, *_, c=c: (i, c)),
                     pl.BlockSpec((BLK, ATT_KVW), lambda i, *_, c=c: nxt(i) + (c,))]
    grid_spec = pltpu.PrefetchScalarGridSpec(
        num_scalar_prefetch=2,
        grid=(nb // ATT_QB,),
        in_specs=[
            pl.BlockSpec(memory_space=pltpu.SMEM),
            pl.BlockSpec((tq, ATT_Q), lambda i, *_: (i, CB_AQ)),
            pl.BlockSpec((tq, ATT_Q), lambda i, *_: (i, CB_AZ)),
        ] + kv_specs,
        out_specs=pl.BlockSpec((tq, ATT_Q), lambda i, *_: (i, 0)),
    )
    return pl.pallas_call(
        _attn_body,
        grid_spec=grid_spec,
        out_shape=jax.ShapeDtypeStruct((n, ATT_Q), BF16),
        compiler_params=_cparams("parallel"),
        name="attention",
    )(pos_tab, nblk_tab, sink[layer], proj, proj, proj, proj, proj, proj, proj, proj)


def _l2norm_heads(x):
    parts = []
    for hh in range(DN_H):
        xh = x[:, hh * HD:(hh + 1) * HD]
        parts.append(xh * lax.rsqrt(jnp.sum(xh * xh, axis=-1, keepdims=True) + EPS))
    return parts


def _dn_prep_body(dq, dqp, dqn, dk, dkp, dkn, dv, dvp, dvn, w_ref, valid_ref,
                  q_ref, k_ref, v_ref, buf_ref):
    def conv_silu(cur, prev, nxt, w0):
        x = cur[...].astype(F32)
        x_m1, x_p1 = _shifted(buf_ref, x, prev[...].astype(F32), nxt[...].astype(F32))
        c = (x_m1 * w_ref[0:1, w0:w0 + DN_W] + x * w_ref[1:2, w0:w0 + DN_W]
             + x_p1 * w_ref[2:3, w0:w0 + DN_W])
        return _silu(c)

    q = conv_silu(dq, dqp, dqn, 0)
    for hh, qh in enumerate(_l2norm_heads(q)):
        q_ref[:, hh * HD:(hh + 1) * HD] = (qh * (HD ** -0.5)).astype(BF16)
    k = conv_silu(dk, dkp, dkn, DN_W)
    valid = valid_ref[...]
    for hh, kh in enumerate(_l2norm_heads(k)):
        k_ref[:, hh * HD:(hh + 1) * HD] = (kh * valid).astype(BF16)
    v_ref[...] = conv_silu(dv, dvp, dvn, 2 * DN_W).astype(BF16)


def _dn_prep(proj, dn_conv_w, valid, layer):
    n = proj.shape[0]
    out = jax.ShapeDtypeStruct((n, DN_W), BF16)
    ospec = pl.BlockSpec((TM, DN_W), lambda i: (i, 0))
    return pl.pallas_call(
        _dn_prep_body,
        grid=(n // TM,),
        in_specs=_halo_specs(DN_W, CB_DQ, n) + _halo_specs(DN_W, CB_DK, n)
        + _halo_specs(DN_W, CB_DV, n) + [
            pl.BlockSpec((None, 3, 3 * DN_W), lambda i: (layer, 0, 0)),
            pl.BlockSpec((TM, 1), lambda i: (i, 0)),
        ],
        out_specs=[ospec, ospec, ospec],
        out_shape=[out, out, out],
        scratch_shapes=[pltpu.VMEM((TM + 2 * HALO, DN_W), F32)],
        compiler_params=_cparams("parallel"),
        name="dn_prep",
    )(proj, proj, proj, proj, proj, proj, proj, proj, proj, dn_conv_w, valid)


def _dn_chunks(chains, r_i, c_i, s_ref):
    xor_idx = r_i ^ c_i
    st = []
    for q, k, v, beta, col, row, tot, lower, idx in chains:
        incl, strict = (r_i >= c_i, r_i > c_i) if lower else (r_i <= c_i, r_i < c_i)
        decay = jnp.exp(jnp.where(incl, col - row, -jnp.inf))
        kbeta = k * beta
        gram = _dot_nt(jnp.concatenate([kbeta, q], axis=0).astype(BF16), k.astype(BF16))
        p = -jnp.where(strict, gram[:BLK] * decay, 0.0)
        ecol = jnp.exp(col)
        st.append(dict(
            p=p, lower=lower, idx=idx, tot=tot, qk=(gram[BLK:] * decay).astype(BF16),
            rhs=jnp.concatenate([v * beta, kbeta * ecol], axis=1).astype(BF16),
            qg=(q * ecol).astype(BF16),
            k_tail_t=(k * jnp.exp(tot - col)).T.astype(BF16),
            t=(xor_idx == 0).astype(F32) + jnp.where(xor_idx == 1, p, 0.0)))
    s = 2
    while s < BLK:
        for c in st:
            c["tb"] = c["t"].astype(BF16)
            link = jnp.where((xor_idx >= s) & (xor_idx < 2 * s), c["p"], 0.0).astype(BF16)
            c["lt"] = _dot(link, c["tb"]).astype(BF16)
        for c in st:
            c["t"] = c["t"] + _dot(c["tb"], c["lt"])
        s *= 2
    for c in st:
        c["x"] = _dot(c["t"].astype(BF16), c["rhs"])
    for c in st:
        c["s"] = s_ref[c["idx"]]
        c["ws"] = _dot(jnp.concatenate([c["x"][:, HD:].astype(BF16), c["qg"]], axis=0),
                       c["s"].astype(BF16))
    for c in st:
        v_new = c["x"][:, :HD] - c["ws"][:BLK]
        c["r"] = _dot(jnp.concatenate([c["qk"], c["k_tail_t"]], axis=0), v_new.astype(BF16))
    outs = []
    for c in st:
        s_ref[c["idx"]] = c["s"] * jnp.exp(c["tot"]) + c["r"][BLK:]
        outs.append(c["ws"][BLK:] + c["r"][:BLK])
    return outs


def _dn_scan_body(fblk_ref, bblk_ref, first_ref,
                  qf, kf, vf, gf, gtf, qb, kb, vb, gb, gtb, of_ref, ob_ref, s_ref):
    i = pl.program_id(0)

    @pl.when(first_ref[i] == 1)
    def _():
        s_ref[...] = jnp.zeros_like(s_ref)

    r_i = lax.broadcasted_iota(jnp.int32, (BLK, BLK), 0)
    c_i = lax.broadcasted_iota(jnp.int32, (BLK, BLK), 1)
    lower = (r_i >= c_i).astype(F32)
    upper = (r_i <= c_i).astype(F32)
    chains, dests = [], []
    for d, (q_ref, k_ref, v_ref, g_ref, gt_ref, o_ref) in enumerate(
            ((qf, kf, vf, gf, gtf, of_ref), (qb, kb, vb, gb, gtb, ob_ref))):
        if d == 0:
            csum, csum_t, last = lower, upper, BLK - 1
        else:
            csum, csum_t, last = upper, lower, 0
        gates = g_ref[...]
        gc = jnp.dot(csum, gates, preferred_element_type=F32, precision=HIGHEST)
        gct = jnp.dot(gt_ref[...], csum_t, preferred_element_type=F32, precision=HIGHEST)
        for hh in range(DN_H):
            hs = slice(hh * HD, (hh + 1) * HD)
            cb = d * DN_H + hh
            cg = 2 * DN_H + cb
            chains.append((q_ref[:, hs].astype(F32), k_ref[:, hs].astype(F32), v_ref[:, hs].astype(F32),
                           gates[:, cb:cb + 1], gc[:, cg:cg + 1], gct[cg:cg + 1, :],
                           gc[last:last + 1, cg:cg + 1], d == 0, cb))
            dests.append((o_ref, hs))
    for (o_ref, hs), o in zip(dests, _dn_chunks(chains, r_i, c_i, s_ref)):
        o_ref[:, hs] = o.astype(BF16)


def _dn_scan(q, k, v, gates, gates_t, fblk, bblk, first):
    n = q.shape[0]
    steps = fblk.shape[0]
    fmap = lambda i, f, b, s: (f[i], 0)
    bmap = lambda i, f, b, s: (b[i], 0)
    fmap_t = lambda i, f, b, s: (0, f[i])
    bmap_t = lambda i, f, b, s: (0, b[i])
    wide = lambda m: pl.BlockSpec((BLK, DN_W), m)
    grid_spec = pltpu.PrefetchScalarGridSpec(
        num_scalar_prefetch=3,
        grid=(steps,),
        in_specs=[wide(fmap), wide(fmap), wide(fmap),
                  pl.BlockSpec((BLK, BLK), fmap), pl.BlockSpec((BLK, BLK), fmap_t),
                  wide(bmap), wide(bmap), wide(bmap),
                  pl.BlockSpec((BLK, BLK), bmap), pl.BlockSpec((BLK, BLK), bmap_t)],
        out_specs=[wide(fmap), wide(bmap)],
        scratch_shapes=[pltpu.VMEM((2 * DN_H, HD, HD), F32)],
    )
    out = jax.ShapeDtypeStruct((n, DN_W), BF16)
    return pl.pallas_call(
        _dn_scan_body,
        grid_spec=grid_spec,
        out_shape=[out, out],
        compiler_params=_cparams("arbitrary"),
        name="dn_scan",
    )(fblk, bblk, first, q, k, v, gates, gates_t, q, k, v, gates, gates_t)


def _out_proj_body(h_ref, yc_ref, ya_ref, of_ref, ob_ref, dz_ref, nw_ref, valid_ref, w_ref, o_ref):
    parts = []
    for hh in range(DN_H):
        hs = slice(hh * HD, (hh + 1) * HD)
        o = of_ref[:, hs].astype(F32) + ob_ref[:, hs].astype(F32)
        ms = jnp.mean(o * o, axis=-1, keepdims=True)
        on = o * lax.rsqrt(ms + EPS) * nw_ref[...]
        parts.append((on * _silu(dz_ref[:, hs].astype(F32))).astype(BF16))
    y_dn = jnp.concatenate(parts, axis=1)
    c0, c1 = CONV_W, CONV_W + ATT_Q
    acc = (_dot(yc_ref[...], w_ref[0:c0, :]) + _dot(ya_ref[...], w_ref[c0:c1, :])
           + _dot(y_dn, w_ref[c1:, :]))
    o_ref[...] = h_ref[...] + jnp.where(valid_ref[...] > 0.0, acc, 0.0)


def _out_proj(h, y_conv, y_att, o_f, o_b, proj, dn_norm_w, valid, w_out, layer):
    n = h.shape[0]
    row = lambda width: pl.BlockSpec((TM, width), lambda i: (i, 0))
    return pl.pallas_call(
        _out_proj_body,
        grid=(n // TM,),
        in_specs=[row(D_MODEL), row(CONV_W), row(ATT_Q), row(DN_W), row(DN_W),
                  pl.BlockSpec((TM, DN_W), lambda i: (i, CB_DZ)),
                  pl.BlockSpec((None, 1, HD), lambda i: (layer, 0, 0)),
                  row(1),
                  pl.BlockSpec((None, D_MODEL, D_MODEL), lambda i: (layer, 0, 0))],
        out_specs=row(D_MODEL),
        out_shape=jax.ShapeDtypeStruct((n, D_MODEL), F32),
        compiler_params=_cparams("parallel"),
        name="out_proj",
    )(h, y_conv, y_att, o_f, o_b, proj, dn_norm_w, valid, w_out)


def _final_norm_body(h_ref, nw_ref, o_ref):
    x = h_ref[...]
    ms = jnp.mean(x * x, axis=-1, keepdims=True)
    o_ref[...] = x * lax.rsqrt(ms + EPS) * nw_ref[...]


def _final_norm(h, w, start_block, batch, seq):
    nb = (BLK + seq) // BLK
    return pl.pallas_call(
        _final_norm_body,
        grid=(batch, seq // BLK),
        in_specs=[pl.BlockSpec((BLK, D_MODEL), lambda b, j: (start_block + b * nb + 1 + j, 0)),
                  pl.BlockSpec((1, D_MODEL), lambda b, j: (0, 0))],
        out_specs=pl.BlockSpec((None, BLK, D_MODEL), lambda b, j: (b, j, 0)),
        out_shape=jax.ShapeDtypeStruct((batch, seq, D_MODEL), F32),
        compiler_params=_cparams("parallel", "parallel"),
        name="final_norm",
    )(h, w)


W_COLS = 256


def _regroup_body(perm_ref, w_ref, o_ref):
    o_ref[...] = w_ref[...].astype(BF16)


def _regroup_w_in(w_in, perm):
    depth = w_in.shape[0]
    grid_spec = pltpu.PrefetchScalarGridSpec(
        num_scalar_prefetch=1,
        grid=(depth, N_BIG // W_COLS),
        in_specs=[pl.BlockSpec((None, D_MODEL, W_COLS), lambda l, j, perm: (l, 0, perm[j]))],
        out_specs=pl.BlockSpec((None, D_MODEL, W_COLS), lambda l, j, perm: (l, 0, j)),
    )
    return pl.pallas_call(
        _regroup_body,
        grid_spec=grid_spec,
        out_shape=jax.ShapeDtypeStruct((depth, D_MODEL, N_BIG), BF16),
        compiler_params=_cparams("parallel", "parallel"),
        name="regroup_w_in",
    )(perm, w_in)


def _tables(seq_blocks, n_blocks):
    pos, nblk, fblk, bblk, first = [], [], [], [], []
    start = 0
    for nb in seq_blocks:
        pos += list(range(nb))
        nblk += [nb] * nb
        fblk += [start + j for j in range(nb)]
        bblk += [start + nb - 1 - j for j in range(nb)]
        first += [1] + [0] * (nb - 1)
        start += nb
    pos += [0] * (n_blocks - start)
    nblk += [1] * (n_blocks - start)
    as_i32 = lambda a: jnp.asarray(np.asarray(a, np.int32))
    return as_i32(pos), as_i32(nblk), as_i32(fblk), as_i32(bblk), as_i32(first)


def _forward(xs, meta_tokens, norm_w, w_in, conv_a_w, attn_sink, dn_conv_w, dn_a_log, dn_dt_bias,
             dn_norm_w, w_out, final_norm_w):
    depth = w_in.shape[0]
    lead = jnp.concatenate([jnp.zeros((PAD, D_MODEL), F32), meta_tokens.astype(F32)], axis=0)
    rows, seq_blocks = [], []
    for x in xs:
        for bi in range(x.shape[0]):
            rows += [lead, x[bi]]
            seq_blocks.append((BLK + x.shape[1]) // BLK)
    n_real = sum(seq_blocks) * BLK
    n = -(-n_real // TM_PROJ) * TM_PROJ
    rows.append(jnp.zeros((n - n_real, D_MODEL), F32))
    h = jnp.concatenate(rows, axis=0)

    valid_np = np.zeros((n, 1), np.float32)
    start = 0
    for nb in seq_blocks:
        valid_np[start + PAD:start + nb * BLK] = 1.0
        start += nb * BLK
    valid = jnp.asarray(valid_np)
    pos_tab, nblk_tab, fblk, bblk, first = _tables(seq_blocks, n // BLK)

    c = np.cumsum([0, CONV_W, CONV_W, CONV_W, CONV_W, ATT_Q, ATT_KVW, ATT_KVW, ATT_Q,
                   DN_W, DN_W, DN_W, DN_W, 2 * DN_H, 2 * DN_H])
    order = [(c[4], c[5]), (c[7], c[8]), (c[8], c[12]), (c[5], c[7]), (c[0], c[4])]
    perm = np.concatenate([np.arange(a // W_COLS, b // W_COLS) for a, b in order]).astype(np.int32)
    w_big = _regroup_w_in(w_in, jnp.asarray(perm))
    w_small = jnp.pad(w_in[:, :, c[12]:c[14]], ((0, 0), (0, 0), (0, BLK - N_GATE))).astype(BF16)
    w_small_t = jnp.swapaxes(w_small, 1, 2)
    pad_gate = lambda a: jnp.pad(a.reshape(depth, 2 * DN_H), ((0, 0), (2 * DN_H, BLK - N_GATE)))
    a_row = pad_gate(dn_a_log.astype(F32))[:, None, :]
    b_row = pad_gate(dn_dt_bias.astype(F32))[:, None, :]
    a_col, b_col = jnp.swapaxes(a_row, 1, 2), jnp.swapaxes(b_row, 1, 2)
    w_out_b = w_out.astype(BF16)
    norm_w3 = norm_w[:, None, :]
    dn_norm_w3 = dn_norm_w[:, None, :]

    for l in range(depth):
        proj, gates, gates_t = _in_proj(h, norm_w3, w_big, w_small, w_small_t,
                                        a_row, b_row, a_col, b_col, l)
        y_conv = _conv_branch(proj, conv_a_w, l)
        y_att = _attention(proj, attn_sink, pos_tab, nblk_tab, l)
        q, k, v = _dn_prep(proj, dn_conv_w, valid, l)
        o_f, o_b = _dn_scan(q, k, v, gates, gates_t, fblk, bblk, first)
        h = _out_proj(h, y_conv, y_att, o_f, o_b, proj, dn_norm_w3, valid, w_out_b, l)

    ys, start = [], 0
    for x in xs:
        b, s = x.shape[0], x.shape[1]
        ys.append(_final_norm(h, final_norm_w[None, :], start, b, s))
        start += b * (BLK + s) // BLK
    return tuple(ys)


def kernel(x_prompt, x_sample, meta_tokens, norm_w, w_in, conv_a_w, attn_sink, dn_conv_w,
           dn_a_log, dn_dt_bias, dn_norm_w, w_out, final_norm_w):
    return _forward([x_prompt, x_sample], meta_tokens, norm_w, w_in, conv_a_w, attn_sink, dn_conv_w,
                    dn_a_log, dn_dt_bias, dn_norm_w, w_out, final_norm_w)
```

```python
import numpy as np
import jax
import jax.numpy as jnp
from jax import lax
from jax.experimental import pallas as pl
from jax.experimental.pallas import tpu as pltpu

F32 = jnp.float32
BF16 = jnp.bfloat16
HIGHEST = lax.Precision.HIGHEST

D_MODEL = 2048
N_META = 16
BLK = 128
PAD = BLK - N_META
CONV_W = 512
HD = 128
ATT_HQ = 6
ATT_KV = 2
ATT_G = ATT_HQ // ATT_KV
ATT_Q = ATT_HQ * HD
ATT_KVW = ATT_KV * HD
DN_H = 6
DN_W = DN_H * HD
N_GATE = 4 * DN_H
EPS = 1e-6

N_DN = 4 * DN_W
N_ATT = 2 * ATT_Q + 2 * ATT_KVW
N_CONV = 4 * CONV_W
N_BIG = N_DN + N_ATT + N_CONV
CONV_HALF = CONV_W // 2
W_COLS = 256

TM_PROJ = 1024
TN_PROJ = 1024
ROW_CHUNK = 128
TM = 512
HALO = 8
HALO_IN = 16
VMEM_LIMIT = 56 * 1024 * 1024


def _cparams(*sem):
    return pltpu.CompilerParams(dimension_semantics=sem, vmem_limit_bytes=VMEM_LIMIT)


def _silu(x):
    return x * (1.0 / (1.0 + jnp.exp(-x)))


def _dot(a, b):
    return jnp.dot(a, b, preferred_element_type=F32)


def _dot_nt(a, b, precision=None):
    return lax.dot_general(a, b, (((1,), (1,)), ((), ())), preferred_element_type=F32,
                           precision=precision)


def _softplus(x):
    return jnp.maximum(x, 0.0) + jnp.log1p(jnp.exp(-jnp.abs(x)))


def _gate_fn(x, idx, a_log, bias):
    beta = 1.0 / (1.0 + jnp.exp(-x))
    g = -jnp.exp(a_log) * _softplus(x + bias)
    return jnp.where(idx < 2 * DN_H, beta, jnp.where(idx < N_GATE, g, 0.0))


def _proj_att_body(h_ref, nw_ref, w_ref, ws_ref, wst_ref, arow_ref, brow_ref, acol_ref, bcol_ref,
                   o_ref, g_ref, gt_ref, xn_ref):
    @pl.when(pl.program_id(1) == 0)
    def _():
        for r in range(0, TM_PROJ, ROW_CHUNK):
            x = h_ref[r:r + ROW_CHUNK, :]
            ms = jnp.mean(x * x, axis=-1, keepdims=True)
            xn_ref[r:r + ROW_CHUNK, :] = (x * lax.rsqrt(ms + EPS) * nw_ref[...]).astype(BF16)
        p = _dot(xn_ref[...], ws_ref[...])
        pt = _dot_nt(wst_ref[...], xn_ref[...])
        g_ref[...] = _gate_fn(p, lax.broadcasted_iota(jnp.int32, p.shape, 1), arow_ref[...], brow_ref[...])
        gt_ref[...] = _gate_fn(pt, lax.broadcasted_iota(jnp.int32, pt.shape, 0), acol_ref[...], bcol_ref[...])

    o_ref[...] = _dot(xn_ref[...], w_ref[...]).astype(BF16)


def _proj_att(h, norm_w, w_big, w_small, w_small_t, arow, brow, acol, bcol, layer):
    n = h.shape[0]
    lay3 = lambda i, j: (layer, 0, 0)
    return pl.pallas_call(
        _proj_att_body,
        grid=(n // TM_PROJ, N_ATT // TN_PROJ),
        in_specs=[
            pl.BlockSpec((TM_PROJ, D_MODEL), lambda i, j: (i, 0)),
            pl.BlockSpec((None, 1, D_MODEL), lay3),
            pl.BlockSpec((None, D_MODEL, TN_PROJ), lambda i, j: (layer, 0, N_DN // TN_PROJ + j)),
            pl.BlockSpec((None, D_MODEL, BLK), lay3),
            pl.BlockSpec((None, BLK, D_MODEL), lay3),
            pl.BlockSpec((None, 1, BLK), lay3),
            pl.BlockSpec((None, 1, BLK), lay3),
            pl.BlockSpec((None, BLK, 1), lay3),
            pl.BlockSpec((None, BLK, 1), lay3),
        ],
        out_specs=[pl.BlockSpec((TM_PROJ, TN_PROJ), lambda i, j: (i, j)),
                   pl.BlockSpec((TM_PROJ, BLK), lambda i, j: (i, 0)),
                   pl.BlockSpec((BLK, TM_PROJ), lambda i, j: (0, i)),
                   pl.BlockSpec((TM_PROJ, D_MODEL), lambda i, j: (i, 0))],
        out_shape=[jax.ShapeDtypeStruct((n, N_ATT), BF16),
                   jax.ShapeDtypeStruct((n, BLK), F32), jax.ShapeDtypeStruct((BLK, n), F32),
                   jax.ShapeDtypeStruct((n, D_MODEL), BF16)],
        compiler_params=_cparams("parallel", "arbitrary"),
        name="proj_att",
    )(h, norm_w, w_big, w_small, w_small_t, arow, brow, acol, bcol)


def _xn_specs(n):
    per = TM_PROJ // HALO_IN
    last = n // HALO_IN - 1
    return [pl.BlockSpec((TM_PROJ, D_MODEL), lambda i, j: (i, 0)),
            pl.BlockSpec((HALO_IN, D_MODEL), lambda i, j: (jnp.maximum(i * per - 1, 0), 0)),
            pl.BlockSpec((HALO_IN, D_MODEL), lambda i, j: (jnp.minimum((i + 1) * per, last), 0))]


def _edge_flags():
    i = pl.program_id(0)
    return (i > 0).astype(F32), (i < pl.num_programs(0) - 1).astype(F32)


def _chunked(compute, epilogue):
    chunks = TM_PROJ // ROW_CHUNK
    for c in range(chunks):
        compute(c)
        if c > 0:
            epilogue(c - 1)
    epilogue(chunks - 1)


def _proj_dn_body(xn_ref, prev_ref, next_ref, w_ref, cw_ref, valid_ref, o_ref, acc_ref):
    j = pl.program_id(1)
    has_prev, has_next = _edge_flags()
    acc_ref[0:HALO, :] = _dot(prev_ref[...], w_ref[...])[HALO_IN - HALO:, :] * has_prev
    acc_ref[HALO + TM_PROJ:, :] = _dot(next_ref[...], w_ref[...])[:HALO, :] * has_next
    is_gate = j == 3
    q_scale = jnp.where(j == 0, HD ** -0.5, 1.0)

    def compute(c):
        r0 = c * ROW_CHUNK
        acc_ref[HALO + r0:HALO + r0 + ROW_CHUNK, :] = _dot(xn_ref[r0:r0 + ROW_CHUNK, :], w_ref[...])

    def epilogue(c):
        r0 = c * ROW_CHUNK
        x = acc_ref[HALO + r0:HALO + r0 + ROW_CHUNK, :]
        x_m1 = acc_ref[HALO + r0 - 1:HALO + r0 - 1 + ROW_CHUNK, :]
        x_p1 = acc_ref[HALO + r0 + 1:HALO + r0 + 1 + ROW_CHUNK, :]
        conv = x_m1 * cw_ref[0:1, :] + x * cw_ref[1:2, :] + x_p1 * cw_ref[2:3, :]
        y = _silu(jnp.where(is_gate, x, conv))
        row_f = jnp.where(j == 1, valid_ref[r0:r0 + ROW_CHUNK, :], 1.0) * q_scale
        for hh in range(DN_H):
            hs = slice(hh * HD, (hh + 1) * HD)
            yh = y[:, hs]
            inv = lax.rsqrt(jnp.sum(yh * yh, axis=-1, keepdims=True) + EPS)
            o_ref[r0:r0 + ROW_CHUNK, hs] = (yh * (jnp.where(j < 2, inv, 1.0) * row_f)).astype(BF16)

    _chunked(compute, epilogue)


def _proj_dn(xn, w_big, dn_conv_w4, valid, layer):
    n = xn.shape[0]
    return pl.pallas_call(
        _proj_dn_body,
        grid=(n // TM_PROJ, N_DN // DN_W),
        in_specs=_xn_specs(n) + [
            pl.BlockSpec((None, D_MODEL, DN_W), lambda i, j: (layer, 0, j)),
            pl.BlockSpec((None, None, 3, DN_W), lambda i, j: (layer, j, 0, 0)),
            pl.BlockSpec((TM_PROJ, 1), lambda i, j: (i, 0)),
        ],
        out_specs=pl.BlockSpec((None, TM_PROJ, DN_W), lambda i, j: (j, i, 0)),
        out_shape=jax.ShapeDtypeStruct((N_DN // DN_W, n, DN_W), BF16),
        scratch_shapes=[pltpu.VMEM((TM_PROJ + 2 * HALO, DN_W), F32)],
        compiler_params=_cparams("parallel", "parallel"),
        name="proj_dn",
    )(xn, xn, xn, w_big, dn_conv_w4, valid)


def _proj_conv_body(xn_ref, prev_ref, next_ref, w_ref, cw_ref, o_ref, u_ref, bz_ref):
    cx, cb, cc, cz = (slice(k * CONV_HALF, (k + 1) * CONV_HALF) for k in range(4))
    has_prev, has_next = _edge_flags()
    pv = _dot(prev_ref[...], w_ref[...])[HALO_IN - HALO:, :]
    nx = _dot(next_ref[...], w_ref[...])[:HALO, :]
    u_ref[0:HALO, :] = pv[:, cc] * pv[:, cx] * has_prev
    u_ref[HALO + TM_PROJ:, :] = nx[:, cc] * nx[:, cx] * has_next

    def compute(c):
        r0 = c * ROW_CHUNK
        acc = _dot(xn_ref[r0:r0 + ROW_CHUNK, :], w_ref[...])
        u_ref[HALO + r0:HALO + r0 + ROW_CHUNK, :] = acc[:, cc] * acc[:, cx]
        bz_ref[r0:r0 + ROW_CHUNK, :] = acc[:, cb] * _silu(acc[:, cz])

    def epilogue(c):
        r0 = c * ROW_CHUNK
        u = u_ref[HALO + r0:HALO + r0 + ROW_CHUNK, :]
        u_m1 = u_ref[HALO + r0 - 1:HALO + r0 - 1 + ROW_CHUNK, :]
        u_p1 = u_ref[HALO + r0 + 1:HALO + r0 + 1 + ROW_CHUNK, :]
        conv = u_m1 * cw_ref[0:1, :] + u * cw_ref[1:2, :] + u_p1 * cw_ref[2:3, :]
        o_ref[r0:r0 + ROW_CHUNK, :] = (bz_ref[r0:r0 + ROW_CHUNK, :] * conv).astype(BF16)

    _chunked(compute, epilogue)


def _proj_conv(xn, w_big, conv_w, layer):
    n = xn.shape[0]
    first = (N_DN + N_ATT) // TN_PROJ
    return pl.pallas_call(
        _proj_conv_body,
        grid=(n // TM_PROJ, N_CONV // TN_PROJ),
        in_specs=_xn_specs(n) + [
            pl.BlockSpec((None, D_MODEL, TN_PROJ), lambda i, j: (layer, 0, first + j)),
            pl.BlockSpec((None, 3, CONV_HALF), lambda i, j: (layer, 0, j)),
        ],
        out_specs=pl.BlockSpec((TM_PROJ, CONV_HALF), lambda i, j: (i, j)),
        out_shape=jax.ShapeDtypeStruct((n, CONV_W), BF16),
        scratch_shapes=[pltpu.VMEM((TM_PROJ + 2 * HALO, CONV_HALF), F32),
                        pltpu.VMEM((TM_PROJ, CONV_HALF), F32)],
        compiler_params=_cparams("parallel", "parallel"),
        name="proj_conv",
    )(xn, xn, xn, w_big, conv_w)


ATT_QB = 2
CB_AQ, CB_AZ = 0, 1
CB_AK, CB_AV = 2 * ATT_Q // ATT_KVW, 2 * ATT_Q // ATT_KVW + 1


def _attn_body(pos_ref, nblk_ref, sink_ref, q_ref, az_ref, kp, kc, kn, vp, vc, vn, o_ref):
    i = pl.program_id(0)
    qi = lax.broadcasted_iota(jnp.int32, (BLK, 3 * BLK), 0)
    kj = lax.broadcasted_iota(jnp.int32, (BLK, 3 * BLK), 1) - BLK
    dist = jnp.abs(qi - kj)
    distf = dist.astype(F32)
    allowed = []
    for j in range(ATT_QB):
        b = i * ATT_QB + j
        kabs = pos_ref[b] * BLK + kj
        allowed.append((dist <= BLK) & (kabs >= PAD) & (kabs < nblk_ref[b] * BLK))

    def band(prev_ref, cur_ref, next_ref, j, cols):
        blocks = ([prev_ref[:, cols]] + [cur_ref[r * BLK:(r + 1) * BLK, cols] for r in range(ATT_QB)]
                  + [next_ref[:, cols]])
        return jnp.concatenate(blocks[j:j + 3], axis=0)

    scores = {}
    for j in range(ATT_QB):
        rows = slice(j * BLK, (j + 1) * BLK)
        for kvh in range(ATT_KV):
            k3 = band(kp, kc, kn, j, slice(kvh * HD, (kvh + 1) * HD))
            q3 = jnp.concatenate([q_ref[rows, (kvh * ATT_G + g) * HD:(kvh * ATT_G + g + 1) * HD]
                                  for g in range(ATT_G)], axis=0)
            scores[j, kvh] = _dot_nt(q3, k3)
    probs, dens = {}, {}
    for j in range(ATT_QB):
        for head in range(ATT_HQ):
            kvh, g = divmod(head, ATT_G)
            slope = float(2.0 ** (-8.0 * (head + 1) / ATT_HQ))
            s = scores[j, kvh][g * BLK:(g + 1) * BLK] * (HD ** -0.5) - slope * distf
            s = jnp.where(allowed[j], s, -jnp.inf)
            sk = sink_ref[head]
            m = jnp.maximum(jnp.max(s, axis=-1, keepdims=True), sk)
            p = jnp.exp(s - m)
            dens[j, head] = jnp.sum(p, axis=-1, keepdims=True) + jnp.exp(sk - m)
            probs[j, head] = p.astype(BF16)
    for j in range(ATT_QB):
        rows = slice(j * BLK, (j + 1) * BLK)
        for kvh in range(ATT_KV):
            v3 = band(vp, vc, vn, j, slice(kvh * HD, (kvh + 1) * HD))
            p3 = jnp.concatenate([probs[j, kvh * ATT_G + g] for g in range(ATT_G)], axis=0)
            o3 = _dot(p3, v3)
            for g in range(ATT_G):
                head = kvh * ATT_G + g
                hs = slice(head * HD, (head + 1) * HD)
                o = o3[g * BLK:(g + 1) * BLK] / dens[j, head]
                o_ref[rows, hs] = (o * _silu(az_ref[rows, hs].astype(F32))).astype(BF16)


def _attention(proj, sink, pos_tab, nblk_tab, layer):
    n = proj.shape[0]
    nb = n // BLK
    tq = ATT_QB * BLK
    prev = lambda i, *_: (jnp.maximum(i * ATT_QB - 1, 0),)
    nxt = lambda i, *_: (jnp.minimum((i + 1) * ATT_QB, nb - 1),)
    kv_specs = []
    for c in (CB_AK, CB_AV):
        kv_specs += [pl.BlockSpec((BLK, ATT_KVW), lambda i, *_, c=c: prev(i) + (c,)),
                     pl.BlockSpec((tq, ATT_KVW), lambda i, *_, c=c: (i, c)),
                     pl.BlockSpec((BLK, ATT_KVW), lambda i, *_, c=c: nxt(i) + (c,))]
    grid_spec = pltpu.PrefetchScalarGridSpec(
        num_scalar_prefetch=2,
        grid=(nb // ATT_QB,),
        in_specs=[
            pl.BlockSpec(memory_space=pltpu.SMEM),
            pl.BlockSpec((tq, ATT_Q), lambda i, *_: (i, CB_AQ)),
            pl.BlockSpec((tq, ATT_Q), lambda i, *_: (i, CB_AZ)),
        ] + kv_specs,
        out_specs=pl.BlockSpec((tq, ATT_Q), lambda i, *_: (i, 0)),
    )
    return pl.pallas_call(
        _attn_body,
        grid_spec=grid_spec,
        out_shape=jax.ShapeDtypeStruct((n, ATT_Q), BF16),
        compiler_params=_cparams("parallel"),
        name="attention",
    )(pos_tab, nblk_tab, sink[layer], proj, proj, proj, proj, proj, proj, proj, proj)


def _dn_chunks(chains, r_i, c_i, s_ref):
    xor_idx = r_i ^ c_i
    st = []
    for q, k, v, beta, col, row, tot, lower, idx in chains:
        incl, strict = (r_i >= c_i, r_i > c_i) if lower else (r_i <= c_i, r_i < c_i)
        decay = jnp.exp(jnp.where(incl, col - row, -jnp.inf))
        kbeta = k * beta
        gram = _dot_nt(jnp.concatenate([kbeta, q], axis=0).astype(BF16), k.astype(BF16))
        p = -jnp.where(strict, gram[:BLK] * decay, 0.0)
        ecol = jnp.exp(col)
        st.append(dict(
            p=p, idx=idx, tot=tot, qk=(gram[BLK:] * decay).astype(BF16),
            rhs=jnp.concatenate([v * beta, kbeta * ecol], axis=1).astype(BF16),
            qg=(q * ecol).astype(BF16),
            k_tail_t=(k * jnp.exp(tot - col)).T.astype(BF16),
            t=(xor_idx == 0).astype(F32) + jnp.where(xor_idx == 1, p, 0.0)))
    s = 2
    while s < BLK:
        for c in st:
            c["tb"] = c["t"].astype(BF16)
            link = jnp.where((xor_idx >= s) & (xor_idx < 2 * s), c["p"], 0.0).astype(BF16)
            c["tl"] = _dot(c["tb"], link).astype(BF16)
        for c in st:
            c["t"] = c["t"] + _dot(c["tl"], c["tb"])
        s *= 2
    for c in st:
        c["x"] = _dot(c["t"].astype(BF16), c["rhs"])
    for c in st:
        c["s"] = s_ref[c["idx"]]
        c["ws"] = _dot(jnp.concatenate([c["x"][:, HD:].astype(BF16), c["qg"]], axis=0),
                       c["s"].astype(BF16))
    for c in st:
        v_new = c["x"][:, :HD] - c["ws"][:BLK]
        c["r"] = _dot(jnp.concatenate([c["qk"], c["k_tail_t"]], axis=0), v_new.astype(BF16))
    outs = []
    for c in st:
        s_ref[c["idx"]] = c["s"] * jnp.exp(c["tot"]) + c["r"][BLK:]
        outs.append(c["ws"][BLK:] + c["r"][:BLK])
    return outs


def _dn_scan_body(fblk_ref, bblk_ref, first_ref,
                  qf, kf, vf, gf, gtf, qb, kb, vb, gb, gtb, of_ref, ob_ref, s_ref):
    i = pl.program_id(0)

    @pl.when(first_ref[i] == 1)
    def _():
        s_ref[...] = jnp.zeros_like(s_ref)

    r_i = lax.broadcasted_iota(jnp.int32, (BLK, BLK), 0)
    c_i = lax.broadcasted_iota(jnp.int32, (BLK, BLK), 1)
    lower = (r_i >= c_i).astype(F32)
    upper = (r_i <= c_i).astype(F32)
    chains, dests = [], []
    for d, (q_ref, k_ref, v_ref, g_ref, gt_ref, o_ref) in enumerate(
            ((qf, kf, vf, gf, gtf, of_ref), (qb, kb, vb, gb, gtb, ob_ref))):
        if d == 0:
            csum, csum_t, last = lower, upper, BLK - 1
        else:
            csum, csum_t, last = upper, lower, 0
        gates = g_ref[...]
        gc = jnp.dot(csum, gates, preferred_element_type=F32, precision=HIGHEST)
        gct = jnp.dot(gt_ref[...], csum_t, preferred_element_type=F32, precision=HIGHEST)
        for hh in range(DN_H):
            hs = slice(hh * HD, (hh + 1) * HD)
            cb = d * DN_H + hh
            cg = 2 * DN_H + cb
            chains.append((q_ref[:, hs].astype(F32), k_ref[:, hs].astype(F32), v_ref[:, hs].astype(F32),
                           gates[:, cb:cb + 1], gc[:, cg:cg + 1], gct[cg:cg + 1, :],
                           gc[last:last + 1, cg:cg + 1], d == 0, cb))
            dests.append((o_ref, hs))
    for (o_ref, hs), o in zip(dests, _dn_chunks(chains, r_i, c_i, s_ref)):
        o_ref[:, hs] = o.astype(BF16)


def _dn_scan(qkvz, gates, gates_t, fblk, bblk, first):
    n = qkvz.shape[1]
    steps = fblk.shape[0]
    fmap = lambda i, f, b, s: (f[i], 0)
    bmap = lambda i, f, b, s: (b[i], 0)
    fmap_t = lambda i, f, b, s: (0, f[i])
    bmap_t = lambda i, f, b, s: (0, b[i])
    fwide = [pl.BlockSpec((None, BLK, DN_W), lambda i, f, b, s, a=a: (a, f[i], 0)) for a in range(3)]
    bwide = [pl.BlockSpec((None, BLK, DN_W), lambda i, f, b, s, a=a: (a, b[i], 0)) for a in range(3)]
    grid_spec = pltpu.PrefetchScalarGridSpec(
        num_scalar_prefetch=3,
        grid=(steps,),
        in_specs=fwide + [pl.BlockSpec((BLK, BLK), fmap), pl.BlockSpec((BLK, BLK), fmap_t)]
        + bwide + [pl.BlockSpec((BLK, BLK), bmap), pl.BlockSpec((BLK, BLK), bmap_t)],
        out_specs=[pl.BlockSpec((BLK, DN_W), fmap), pl.BlockSpec((BLK, DN_W), bmap)],
        scratch_shapes=[pltpu.VMEM((2 * DN_H, HD, HD), F32)],
    )
    out = jax.ShapeDtypeStruct((n, DN_W), BF16)
    return pl.pallas_call(
        _dn_scan_body,
        grid_spec=grid_spec,
        out_shape=[out, out],
        compiler_params=_cparams("arbitrary"),
        name="dn_scan",
    )(fblk, bblk, first, qkvz, qkvz, qkvz, gates, gates_t, qkvz, qkvz, qkvz, gates, gates_t)


def _out_proj_body(h_ref, yc_ref, ya_ref, of_ref, ob_ref, zs_ref, nw_ref, valid_ref, w_ref, o_ref):
    parts = []
    for hh in range(DN_H):
        hs = slice(hh * HD, (hh + 1) * HD)
        o = of_ref[:, hs].astype(F32) + ob_ref[:, hs].astype(F32)
        ms = jnp.mean(o * o, axis=-1, keepdims=True)
        on = o * lax.rsqrt(ms + EPS) * nw_ref[...]
        parts.append((on * zs_ref[:, hs].astype(F32)).astype(BF16))
    y_dn = jnp.concatenate(parts, axis=1)
    c0, c1 = CONV_W, CONV_W + ATT_Q
    acc = (_dot(yc_ref[...], w_ref[0:c0, :]) + _dot(ya_ref[...], w_ref[c0:c1, :])
           + _dot(y_dn, w_ref[c1:, :]))
    o_ref[...] = h_ref[...] + jnp.where(valid_ref[...] > 0.0, acc, 0.0)


def _out_proj(h, y_conv, y_att, o_f, o_b, qkvz, dn_norm_w, valid, w_out, layer):
    n = h.shape[0]
    row = lambda width: pl.BlockSpec((TM, width), lambda i: (i, 0))
    return pl.pallas_call(
        _out_proj_body,
        grid=(n // TM,),
        in_specs=[row(D_MODEL), row(CONV_W), row(ATT_Q), row(DN_W), row(DN_W),
                  pl.BlockSpec((None, TM, DN_W), lambda i: (3, i, 0)),
                  pl.BlockSpec((None, 1, HD), lambda i: (layer, 0, 0)),
                  row(1),
                  pl.BlockSpec((None, D_MODEL, D_MODEL), lambda i: (layer, 0, 0))],
        out_specs=row(D_MODEL),
        out_shape=jax.ShapeDtypeStruct((n, D_MODEL), F32),
        compiler_params=_cparams("parallel"),
        name="out_proj",
    )(h, y_conv, y_att, o_f, o_b, qkvz, dn_norm_w, valid, w_out)


def _final_norm_body(h_ref, nw_ref, o_ref):
    x = h_ref[...]
    ms = jnp.mean(x * x, axis=-1, keepdims=True)
    o_ref[...] = x * lax.rsqrt(ms + EPS) * nw_ref[...]


def _final_norm(h, w, start_block, batch, seq):
    nb = (BLK + seq) // BLK
    return pl.pallas_call(
        _final_norm_body,
        grid=(batch, seq // BLK),
        in_specs=[pl.BlockSpec((BLK, D_MODEL), lambda b, j: (start_block + b * nb + 1 + j, 0)),
                  pl.BlockSpec((1, D_MODEL), lambda b, j: (0, 0))],
        out_specs=pl.BlockSpec((None, BLK, D_MODEL), lambda b, j: (b, j, 0)),
        out_shape=jax.ShapeDtypeStruct((batch, seq, D_MODEL), F32),
        compiler_params=_cparams("parallel", "parallel"),
        name="final_norm",
    )(h, w)


def _regroup_body(perm_ref, w_ref, o_ref):
    o_ref[...] = w_ref[...].astype(BF16)


def _regroup_w_in(w_in, perm):
    depth = w_in.shape[0]
    grid_spec = pltpu.PrefetchScalarGridSpec(
        num_scalar_prefetch=1,
        grid=(depth, N_BIG // W_COLS),
        in_specs=[pl.BlockSpec((None, D_MODEL, W_COLS), lambda l, j, perm: (l, 0, perm[j]))],
        out_specs=pl.BlockSpec((None, D_MODEL, W_COLS), lambda l, j, perm: (l, 0, j)),
    )
    return pl.pallas_call(
        _regroup_body,
        grid_spec=grid_spec,
        out_shape=jax.ShapeDtypeStruct((depth, D_MODEL, N_BIG), BF16),
        compiler_params=_cparams("parallel", "parallel"),
        name="regroup_w_in",
    )(perm, w_in)


def _tables(seq_blocks, n_blocks):
    pos, nblk, fblk, bblk, first = [], [], [], [], []
    start = 0
    for nb in seq_blocks:
        pos += list(range(nb))
        nblk += [nb] * nb
        fblk += [start + j for j in range(nb)]
        bblk += [start + nb - 1 - j for j in range(nb)]
        first += [1] + [0] * (nb - 1)
        start += nb
    pos += [0] * (n_blocks - start)
    nblk += [1] * (n_blocks - start)
    as_i32 = lambda a: jnp.asarray(np.asarray(a, np.int32))
    return as_i32(pos), as_i32(nblk), as_i32(fblk), as_i32(bblk), as_i32(first)


def _column_perm():
    c = np.cumsum([0, CONV_W, CONV_W, CONV_W, CONV_W, ATT_Q, ATT_KVW, ATT_KVW, ATT_Q,
                   DN_W, DN_W, DN_W, DN_W]) // W_COLS
    blocks = lambda a: np.arange(c[a], c[a + 1])
    cx, cb, cc, cz, aq, ak, av, az, dq, dk, dv, dz = (blocks(a) for a in range(12))
    halves = CONV_HALF // W_COLS
    conv = [g[s * halves:(s + 1) * halves] for s in range(CONV_W // CONV_HALF) for g in (cx, cb, cc, cz)]
    return np.concatenate([dq, dk, dv, dz, aq, az, ak, av] + conv).astype(np.int32)


def _forward(xs, meta_tokens, norm_w, w_in, conv_a_w, attn_sink, dn_conv_w, dn_a_log, dn_dt_bias,
             dn_norm_w, w_out, final_norm_w):
    depth = w_in.shape[0]
    lead = jnp.concatenate([jnp.zeros((PAD, D_MODEL), F32), meta_tokens.astype(F32)], axis=0)
    rows, seq_blocks = [], []
    for x in xs:
        for bi in range(x.shape[0]):
            rows += [lead, x[bi]]
            seq_blocks.append((BLK + x.shape[1]) // BLK)
    n_real = sum(seq_blocks) * BLK
    n = -(-n_real // TM_PROJ) * TM_PROJ
    rows.append(jnp.zeros((n - n_real, D_MODEL), F32))
    h = jnp.concatenate(rows, axis=0)

    valid_np = np.zeros((n, 1), np.float32)
    start = 0
    for nb in seq_blocks:
        valid_np[start + PAD:start + nb * BLK] = 1.0
        start += nb * BLK
    valid = jnp.asarray(valid_np)
    pos_tab, nblk_tab, fblk, bblk, first = _tables(seq_blocks, n // BLK)

    w_big = _regroup_w_in(w_in, jnp.asarray(_column_perm()))
    w_small = jnp.pad(w_in[:, :, N_BIG:N_BIG + N_GATE], ((0, 0), (0, 0), (0, BLK - N_GATE))).astype(BF16)
    w_small_t = jnp.swapaxes(w_small, 1, 2)
    pad_gate = lambda a: jnp.pad(a.reshape(depth, 2 * DN_H), ((0, 0), (2 * DN_H, BLK - N_GATE)))
    a_row = pad_gate(dn_a_log.astype(F32))[:, None, :]
    b_row = pad_gate(dn_dt_bias.astype(F32))[:, None, :]
    a_col, b_col = jnp.swapaxes(a_row, 1, 2), jnp.swapaxes(b_row, 1, 2)
    dn_conv_w4 = jnp.pad(jnp.swapaxes(dn_conv_w.reshape(depth, 3, 3, DN_W), 1, 2),
                         ((0, 0), (0, 1), (0, 0), (0, 0)))
    w_out_b = w_out.astype(BF16)
    norm_w3 = norm_w[:, None, :]
    dn_norm_w3 = dn_norm_w[:, None, :]

    for l in range(depth):
        proj_a, gates, gates_t, xn = _proj_att(h, norm_w3, w_big, w_small, w_small_t,
                                               a_row, b_row, a_col, b_col, l)
        qkvz = _proj_dn(xn, w_big, dn_conv_w4, valid, l)
        y_conv = _proj_conv(xn, w_big, conv_a_w, l)
        y_att = _attention(proj_a, attn_sink, pos_tab, nblk_tab, l)
        o_f, o_b = _dn_scan(qkvz, gates, gates_t, fblk, bblk, first)
        h = _out_proj(h, y_conv, y_att, o_f, o_b, qkvz, dn_norm_w3, valid, w_out_b, l)

    ys, start = [], 0
    for x in xs:
        b, s = x.shape[0], x.shape[1]
        ys.append(_final_norm(h, final_norm_w[None, :], start, b, s))
        start += b * (BLK + s) // BLK
    return tuple(ys)


def kernel(x_prompt, x_sample, meta_tokens, norm_w, w_in, conv_a_w, attn_sink, dn_conv_w,
           dn_a_log, dn_dt_bias, dn_norm_w, w_out, final_norm_w):
    return _forward([x_prompt, x_sample], meta_tokens, norm_w, w_in, conv_a_w, attn_sink, dn_conv_w,
                    dn_a_log, dn_dt_bias, dn_norm_w, w_out, final_norm_w)
```

```python
import numpy as np
import jax
import jax.numpy as jnp
from jax import lax
from jax.experimental import pallas as pl
from jax.experimental.pallas import tpu as pltpu

F32 = jnp.float32
BF16 = jnp.bfloat16
HIGHEST = lax.Precision.HIGHEST

D_MODEL = 2048
N_META = 16
BLK = 128
PAD = BLK - N_META
CONV_W = 512
HD = 128
ATT_HQ = 6
ATT_KV = 2
ATT_G = ATT_HQ // ATT_KV
ATT_Q = ATT_HQ * HD
ATT_KVW = ATT_KV * HD
DN_H = 6
DN_W = DN_H * HD
N_GATE = 4 * DN_H
EPS = 1e-6

N_DN = 4 * DN_W
N_ATT = 2 * ATT_Q + 2 * ATT_KVW
N_CONV = 4 * CONV_W
N_BIG = N_DN + N_ATT + N_CONV
CONV_HALF = CONV_W // 2
W_COLS = 256

TM_PROJ = 1024
TN_PROJ = 1024
ROW_CHUNK = 128
TM = 512
HALO = 8
HALO_IN = 16
VMEM_LIMIT = 56 * 1024 * 1024


def _cparams(*sem):
    return pltpu.CompilerParams(dimension_semantics=sem, vmem_limit_bytes=VMEM_LIMIT)


def _silu(x):
    return x * (1.0 / (1.0 + jnp.exp(-x)))


def _dot(a, b):
    return jnp.dot(a, b, preferred_element_type=F32)


def _dot_nt(a, b, precision=None):
    return lax.dot_general(a, b, (((1,), (1,)), ((), ())), preferred_element_type=F32,
                           precision=precision)


def _softplus(x):
    return jnp.maximum(x, 0.0) + jnp.log1p(jnp.exp(-jnp.abs(x)))


def _gate_fn(x, idx, a_log, bias):
    beta = 1.0 / (1.0 + jnp.exp(-x))
    g = -jnp.exp(a_log) * _softplus(x + bias)
    return jnp.where(idx < 2 * DN_H, beta, jnp.where(idx < N_GATE, g, 0.0))


def _proj_att_body(h_ref, nw_ref, w_ref, ws_ref, wst_ref, arow_ref, brow_ref, acol_ref, bcol_ref,
                   o_ref, g_ref, gt_ref, xn_ref):
    @pl.when(pl.program_id(1) == 0)
    def _():
        for r in range(0, TM_PROJ, ROW_CHUNK):
            x = h_ref[r:r + ROW_CHUNK, :]
            ms = jnp.mean(x * x, axis=-1, keepdims=True)
            xn_ref[r:r + ROW_CHUNK, :] = (x * lax.rsqrt(ms + EPS) * nw_ref[...]).astype(BF16)
        p = _dot(xn_ref[...], ws_ref[...])
        pt = _dot_nt(wst_ref[...], xn_ref[...])
        g_ref[...] = _gate_fn(p, lax.broadcasted_iota(jnp.int32, p.shape, 1), arow_ref[...], brow_ref[...])
        gt_ref[...] = _gate_fn(pt, lax.broadcasted_iota(jnp.int32, pt.shape, 0), acol_ref[...], bcol_ref[...])

    o_ref[...] = _dot(xn_ref[...], w_ref[...]).astype(BF16)


def _proj_att(h, norm_w, w_big, w_small, w_small_t, arow, brow, acol, bcol, layer):
    n = h.shape[0]
    lay3 = lambda i, j: (layer, 0, 0)
    return pl.pallas_call(
        _proj_att_body,
        grid=(n // TM_PROJ, N_ATT // TN_PROJ),
        in_specs=[
            pl.BlockSpec((TM_PROJ, D_MODEL), lambda i, j: (i, 0)),
            pl.BlockSpec((None, 1, D_MODEL), lay3),
            pl.BlockSpec((None, D_MODEL, TN_PROJ), lambda i, j: (layer, 0, N_DN // TN_PROJ + j)),
            pl.BlockSpec((None, D_MODEL, BLK), lay3),
            pl.BlockSpec((None, BLK, D_MODEL), lay3),
            pl.BlockSpec((None, 1, BLK), lay3),
            pl.BlockSpec((None, 1, BLK), lay3),
            pl.BlockSpec((None, BLK, 1), lay3),
            pl.BlockSpec((None, BLK, 1), lay3),
        ],
        out_specs=[pl.BlockSpec((TM_PROJ, TN_PROJ), lambda i, j: (i, j)),
                   pl.BlockSpec((TM_PROJ, BLK), lambda i, j: (i, 0)),
                   pl.BlockSpec((BLK, TM_PROJ), lambda i, j: (0, i)),
                   pl.BlockSpec((TM_PROJ, D_MODEL), lambda i, j: (i, 0))],
        out_shape=[jax.ShapeDtypeStruct((n, N_ATT), BF16),
                   jax.ShapeDtypeStruct((n, BLK), F32), jax.ShapeDtypeStruct((BLK, n), F32),
                   jax.ShapeDtypeStruct((n, D_MODEL), BF16)],
        compiler_params=_cparams("parallel", "arbitrary"),
        name="proj_att",
    )(h, norm_w, w_big, w_small, w_small_t, arow, brow, acol, bcol)


def _xn_specs(n):
    per = TM_PROJ // HALO_IN
    last = n // HALO_IN - 1
    return [pl.BlockSpec((TM_PROJ, D_MODEL), lambda i, j: (i, 0)),
            pl.BlockSpec((HALO_IN, D_MODEL), lambda i, j: (jnp.maximum(i * per - 1, 0), 0)),
            pl.BlockSpec((HALO_IN, D_MODEL), lambda i, j: (jnp.minimum((i + 1) * per, last), 0))]


def _edge_flags():
    i = pl.program_id(0)
    return (i > 0).astype(F32), (i < pl.num_programs(0) - 1).astype(F32)


def _chunked(compute, epilogue):
    chunks = TM_PROJ // ROW_CHUNK
    for c in range(chunks):
        compute(c)
        if c > 0:
            epilogue(c - 1)
    epilogue(chunks - 1)


def _proj_dn_body(xn_ref, prev_ref, next_ref, w_ref, cw_ref, valid_ref, o_ref, acc_ref):
    j = pl.program_id(1)
    has_prev, has_next = _edge_flags()
    acc_ref[0:HALO, :] = _dot(prev_ref[...], w_ref[...])[HALO_IN - HALO:, :] * has_prev
    acc_ref[HALO + TM_PROJ:, :] = _dot(next_ref[...], w_ref[...])[:HALO, :] * has_next
    is_gate = j == 3
    q_scale = jnp.where(j == 0, HD ** -0.5, 1.0)

    def compute(c):
        r0 = c * ROW_CHUNK
        acc_ref[HALO + r0:HALO + r0 + ROW_CHUNK, :] = _dot(xn_ref[r0:r0 + ROW_CHUNK, :], w_ref[...])

    def epilogue(c):
        r0 = c * ROW_CHUNK
        x = acc_ref[HALO + r0:HALO + r0 + ROW_CHUNK, :]
        x_m1 = acc_ref[HALO + r0 - 1:HALO + r0 - 1 + ROW_CHUNK, :]
        x_p1 = acc_ref[HALO + r0 + 1:HALO + r0 + 1 + ROW_CHUNK, :]
        conv = x_m1 * cw_ref[0:1, :] + x * cw_ref[1:2, :] + x_p1 * cw_ref[2:3, :]
        y = _silu(jnp.where(is_gate, x, conv))
        row_f = jnp.where(j == 1, valid_ref[r0:r0 + ROW_CHUNK, :], 1.0) * q_scale
        for hh in range(DN_H):
            hs = slice(hh * HD, (hh + 1) * HD)
            yh = y[:, hs]
            inv = lax.rsqrt(jnp.sum(yh * yh, axis=-1, keepdims=True) + EPS)
            o_ref[r0:r0 + ROW_CHUNK, hs] = (yh * (jnp.where(j < 2, inv, 1.0) * row_f)).astype(BF16)

    _chunked(compute, epilogue)


def _proj_dn(xn, w_big, dn_conv_w4, valid, layer):
    n = xn.shape[0]
    return pl.pallas_call(
        _proj_dn_body,
        grid=(n // TM_PROJ, N_DN // DN_W),
        in_specs=_xn_specs(n) + [
            pl.BlockSpec((None, D_MODEL, DN_W), lambda i, j: (layer, 0, j)),
            pl.BlockSpec((None, None, 3, DN_W), lambda i, j: (layer, j, 0, 0)),
            pl.BlockSpec((TM_PROJ, 1), lambda i, j: (i, 0)),
        ],
        out_specs=pl.BlockSpec((None, TM_PROJ, DN_W), lambda i, j: (j, i, 0)),
        out_shape=jax.ShapeDtypeStruct((N_DN // DN_W, n, DN_W), BF16),
        scratch_shapes=[pltpu.VMEM((TM_PROJ + 2 * HALO, DN_W), F32)],
        compiler_params=_cparams("parallel", "parallel"),
        name="proj_dn",
    )(xn, xn, xn, w_big, dn_conv_w4, valid)


def _proj_conv_body(xn_ref, prev_ref, next_ref, w_ref, cw_ref, o_ref, u_ref, bz_ref):
    cx, cb, cc, cz = (slice(k * CONV_HALF, (k + 1) * CONV_HALF) for k in range(4))
    has_prev, has_next = _edge_flags()
    pv = _dot(prev_ref[...], w_ref[...])[HALO_IN - HALO:, :]
    nx = _dot(next_ref[...], w_ref[...])[:HALO, :]
    u_ref[0:HALO, :] = pv[:, cc] * pv[:, cx] * has_prev
    u_ref[HALO + TM_PROJ:, :] = nx[:, cc] * nx[:, cx] * has_next

    def compute(c):
        r0 = c * ROW_CHUNK
        acc = _dot(xn_ref[r0:r0 + ROW_CHUNK, :], w_ref[...])
        u_ref[HALO + r0:HALO + r0 + ROW_CHUNK, :] = acc[:, cc] * acc[:, cx]
        bz_ref[r0:r0 + ROW_CHUNK, :] = acc[:, cb] * _silu(acc[:, cz])

    def epilogue(c):
        r0 = c * ROW_CHUNK
        u = u_ref[HALO + r0:HALO + r0 + ROW_CHUNK, :]
        u_m1 = u_ref[HALO + r0 - 1:HALO + r0 - 1 + ROW_CHUNK, :]
        u_p1 = u_ref[HALO + r0 + 1:HALO + r0 + 1 + ROW_CHUNK, :]
        conv = u_m1 * cw_ref[0:1, :] + u * cw_ref[1:2, :] + u_p1 * cw_ref[2:3, :]
        o_ref[r0:r0 + ROW_CHUNK, :] = (bz_ref[r0:r0 + ROW_CHUNK, :] * conv).astype(BF16)

    _chunked(compute, epilogue)


def _proj_conv(xn, w_big, conv_w, layer):
    n = xn.shape[0]
    first = (N_DN + N_ATT) // TN_PROJ
    return pl.pallas_call(
        _proj_conv_body,
        grid=(n // TM_PROJ, N_CONV // TN_PROJ),
        in_specs=_xn_specs(n) + [
            pl.BlockSpec((None, D_MODEL, TN_PROJ), lambda i, j: (layer, 0, first + j)),
            pl.BlockSpec((None, 3, CONV_HALF), lambda i, j: (layer, 0, j)),
        ],
        out_specs=pl.BlockSpec((TM_PROJ, CONV_HALF), lambda i, j: (i, j)),
        out_shape=jax.ShapeDtypeStruct((n, CONV_W), BF16),
        scratch_shapes=[pltpu.VMEM((TM_PROJ + 2 * HALO, CONV_HALF), F32),
                        pltpu.VMEM((TM_PROJ, CONV_HALF), F32)],
        compiler_params=_cparams("parallel", "parallel"),
        name="proj_conv",
    )(xn, xn, xn, w_big, conv_w)


ATT_QB = 2
CB_AQ, CB_AZ = 0, 1
CB_AK, CB_AV = 2 * ATT_Q // ATT_KVW, 2 * ATT_Q // ATT_KVW + 1


def _attn_body(pos_ref, nblk_ref, sink_ref, q_ref, az_ref, kp, kc, kn, vp, vc, vn, o_ref):
    i = pl.program_id(0)
    qi = lax.broadcasted_iota(jnp.int32, (BLK, 3 * BLK), 0)
    kj = lax.broadcasted_iota(jnp.int32, (BLK, 3 * BLK), 1) - BLK
    dist = jnp.abs(qi - kj)
    distf = dist.astype(F32)
    allowed = []
    for j in range(ATT_QB):
        b = i * ATT_QB + j
        kabs = pos_ref[b] * BLK + kj
        allowed.append((dist <= BLK) & (kabs >= PAD) & (kabs < nblk_ref[b] * BLK))

    def band(prev_ref, cur_ref, next_ref, j, cols):
        blocks = ([prev_ref[:, cols]] + [cur_ref[r * BLK:(r + 1) * BLK, cols] for r in range(ATT_QB)]
                  + [next_ref[:, cols]])
        return jnp.concatenate(blocks[j:j + 3], axis=0)

    scores = {}
    for j in range(ATT_QB):
        rows = slice(j * BLK, (j + 1) * BLK)
        for kvh in range(ATT_KV):
            k3 = band(kp, kc, kn, j, slice(kvh * HD, (kvh + 1) * HD))
            q3 = jnp.concatenate([q_ref[rows, (kvh * ATT_G + g) * HD:(kvh * ATT_G + g + 1) * HD]
                                  for g in range(ATT_G)], axis=0)
            scores[j, kvh] = _dot_nt(q3, k3)
    probs, dens = {}, {}
    for j in range(ATT_QB):
        for head in range(ATT_HQ):
            kvh, g = divmod(head, ATT_G)
            slope = float(2.0 ** (-8.0 * (head + 1) / ATT_HQ))
            s = scores[j, kvh][g * BLK:(g + 1) * BLK] * (HD ** -0.5) - slope * distf
            s = jnp.where(allowed[j], s, -jnp.inf)
            sk = sink_ref[head]
            m = jnp.maximum(jnp.max(s, axis=-1, keepdims=True), sk)
            p = jnp.exp(s - m)
            dens[j, head] = jnp.sum(p, axis=-1, keepdims=True) + jnp.exp(sk - m)
            probs[j, head] = p.astype(BF16)
    for j in range(ATT_QB):
        rows = slice(j * BLK, (j + 1) * BLK)
        for kvh in range(ATT_KV):
            v3 = band(vp, vc, vn, j, slice(kvh * HD, (kvh + 1) * HD))
            p3 = jnp.concatenate([probs[j, kvh * ATT_G + g] for g in range(ATT_G)], axis=0)
            o3 = _dot(p3, v3)
            for g in range(ATT_G):
                head = kvh * ATT_G + g
                hs = slice(head * HD, (head + 1) * HD)
                o = o3[g * BLK:(g + 1) * BLK] / dens[j, head]
                o_ref[rows, hs] = (o * _silu(az_ref[rows, hs].astype(F32))).astype(BF16)


def _attention(proj, sink, pos_tab, nblk_tab, layer):
    n = proj.shape[0]
    nb = n // BLK
    tq = ATT_QB * BLK
    prev = lambda i, *_: (jnp.maximum(i * ATT_QB - 1, 0),)
    nxt = lambda i, *_: (jnp.minimum((i + 1) * ATT_QB, nb - 1),)
    kv_specs = []
    for c in (CB_AK, CB_AV):
        kv_specs += [pl.BlockSpec((BLK, ATT_KVW), lambda i, *_, c=c: prev(i) + (c,)),
                     pl.BlockSpec((tq, ATT_KVW), lambda i, *_, c=c: (i, c)),
                     pl.BlockSpec((BLK, ATT_KVW), lambda i, *_, c=c: nxt(i) + (c,))]
    grid_spec = pltpu.PrefetchScalarGridSpec(
        num_scalar_prefetch=2,
        grid=(nb // ATT_QB,),
        in_specs=[
            pl.BlockSpec(memory_space=pltpu.SMEM),
            pl.BlockSpec((tq, ATT_Q), lambda i, *_: (i, CB_AQ)),
            pl.BlockSpec((tq, ATT_Q), lambda i, *_: (i, CB_AZ)),
        ] + kv_specs,
        out_specs=pl.BlockSpec((tq, ATT_Q), lambda i, *_: (i, 0)),
    )
    return pl.pallas_call(
        _attn_body,
        grid_spec=grid_spec,
        out_shape=jax.ShapeDtypeStruct((n, ATT_Q), BF16),
        compiler_params=_cparams("parallel"),
        name="attention",
    )(pos_tab, nblk_tab, sink[layer], proj, proj, proj, proj, proj, proj, proj, proj)


def _dn_chunks(chains, r_i, c_i, s_ref):
    xor_idx = r_i ^ c_i
    st = []
    for q, k, v, beta, col, row, tot, lower, idx in chains:
        incl, strict = (r_i >= c_i, r_i > c_i) if lower else (r_i <= c_i, r_i < c_i)
        decay = jnp.exp(jnp.where(incl, col - row, -jnp.inf))
        kbeta = k * beta
        gram = _dot_nt(jnp.concatenate([kbeta, q], axis=0).astype(BF16), k.astype(BF16))
        p = -jnp.where(strict, gram[:BLK] * decay, 0.0)
        ecol = jnp.exp(col)
        st.append(dict(
            p=p, idx=idx, tot=tot, qk=(gram[BLK:] * decay).astype(BF16),
            rhs=jnp.concatenate([v * beta, kbeta * ecol], axis=1).astype(BF16),
            qg=(q * ecol).astype(BF16),
            k_tail_t=(k * jnp.exp(tot - col)).T.astype(BF16),
            t=(xor_idx == 0).astype(F32) + jnp.where(xor_idx == 1, p, 0.0)))
    s = 2
    while s < BLK:
        for c in st:
            c["tb"] = c["t"].astype(BF16)
            link = jnp.where((xor_idx >= s) & (xor_idx < 2 * s), c["p"], 0.0).astype(BF16)
            c["tl"] = _dot(c["tb"], link).astype(BF16)
        for c in st:
            c["t"] = c["t"] + _dot(c["tl"], c["tb"])
        s *= 2
    for c in st:
        c["x"] = _dot(c["t"].astype(BF16), c["rhs"])
    for c in st:
        c["s"] = s_ref[c["idx"]]
        c["ws"] = _dot(jnp.concatenate([c["x"][:, HD:].astype(BF16), c["qg"]], axis=0),
                       c["s"].astype(BF16))
    for c in st:
        v_new = c["x"][:, :HD] - c["ws"][:BLK]
        c["r"] = _dot(jnp.concatenate([c["qk"], c["k_tail_t"]], axis=0), v_new.astype(BF16))
    outs = []
    for c in st:
        s_ref[c["idx"]] = c["s"] * jnp.exp(c["tot"]) + c["r"][BLK:]
        outs.append(c["ws"][BLK:] + c["r"][:BLK])
    return outs


def _dn_scan_body(fblk_ref, bblk_ref, first_ref,
                  qf, kf, vf, gf, gtf, qb, kb, vb, gb, gtb, of_ref, ob_ref, s_ref):
    i = pl.program_id(0)

    @pl.when(first_ref[i] == 1)
    def _():
        s_ref[...] = jnp.zeros_like(s_ref)

    r_i = lax.broadcasted_iota(jnp.int32, (BLK, BLK), 0)
    c_i = lax.broadcasted_iota(jnp.int32, (BLK, BLK), 1)
    lower = (r_i >= c_i).astype(F32)
    upper = (r_i <= c_i).astype(F32)
    chains, dests = [], []
    for d, (q_ref, k_ref, v_ref, g_ref, gt_ref, o_ref) in enumerate(
            ((qf, kf, vf, gf, gtf, of_ref), (qb, kb, vb, gb, gtb, ob_ref))):
        if d == 0:
            csum, csum_t, last = lower, upper, BLK - 1
        else:
            csum, csum_t, last = upper, lower, 0
        gates = g_ref[...]
        gc = jnp.dot(csum, gates, preferred_element_type=F32, precision=HIGHEST)
        gct = jnp.dot(gt_ref[...], csum_t, preferred_element_type=F32, precision=HIGHEST)
        for hh in range(DN_H):
            hs = slice(hh * HD, (hh + 1) * HD)
            cb = d * DN_H + hh
            cg = 2 * DN_H + cb
            chains.append((q_ref[:, hs].astype(F32), k_ref[:, hs].astype(F32), v_ref[:, hs].astype(F32),
                           gates[:, cb:cb + 1], gc[:, cg:cg + 1], gct[cg:cg + 1, :],
                           gc[last:last + 1, cg:cg + 1], d == 0, cb))
            dests.append((o_ref, hs))
    for (o_ref, hs), o in zip(dests, _dn_chunks(chains, r_i, c_i, s_ref)):
        o_ref[:, hs] = o.astype(BF16)


def _dn_scan(qkvz, gates, gates_t, fblk, bblk, first):
    n = qkvz.shape[1]
    steps = fblk.shape[0]
    fmap = lambda i, f, b, s: (f[i], 0)
    bmap = lambda i, f, b, s: (b[i], 0)
    fmap_t = lambda i, f, b, s: (0, f[i])
    bmap_t = lambda i, f, b, s: (0, b[i])
    fwide = [pl.BlockSpec((None, BLK, DN_W), lambda i, f, b, s, a=a: (a, f[i], 0)) for a in range(3)]
    bwide = [pl.BlockSpec((None, BLK, DN_W), lambda i, f, b, s, a=a: (a, b[i], 0)) for a in range(3)]
    grid_spec = pltpu.PrefetchScalarGridSpec(
        num_scalar_prefetch=3,
        grid=(steps,),
        in_specs=fwide + [pl.BlockSpec((BLK, BLK), fmap), pl.BlockSpec((BLK, BLK), fmap_t)]
        + bwide + [pl.BlockSpec((BLK, BLK), bmap), pl.BlockSpec((BLK, BLK), bmap_t)],
        out_specs=[pl.BlockSpec((BLK, DN_W), fmap), pl.BlockSpec((BLK, DN_W), bmap)],
        scratch_shapes=[pltpu.VMEM((2 * DN_H, HD, HD), F32)],
    )
    out = jax.ShapeDtypeStruct((n, DN_W), BF16)
    return pl.pallas_call(
        _dn_scan_body,
        grid_spec=grid_spec,
        out_shape=[out, out],
        compiler_params=_cparams("arbitrary"),
        name="dn_scan",
    )(fblk, bblk, first, qkvz, qkvz, qkvz, gates, gates_t, qkvz, qkvz, qkvz, gates, gates_t)


def _out_proj_body(h_ref, yc_ref, ya_ref, of_ref, ob_ref, zs_ref, nw_ref, valid_ref, w_ref, o_ref):
    parts = []
    for hh in range(DN_H):
        hs = slice(hh * HD, (hh + 1) * HD)
        o = of_ref[:, hs].astype(F32) + ob_ref[:, hs].astype(F32)
        ms = jnp.mean(o * o, axis=-1, keepdims=True)
        on = o * lax.rsqrt(ms + EPS) * nw_ref[...]
        parts.append((on * zs_ref[:, hs].astype(F32)).astype(BF16))
    y_dn = jnp.concatenate(parts, axis=1)
    c0, c1 = CONV_W, CONV_W + ATT_Q
    acc = (_dot(yc_ref[...], w_ref[0:c0, :]) + _dot(ya_ref[...], w_ref[c0:c1, :])
           + _dot(y_dn, w_ref[c1:, :]))
    o_ref[...] = h_ref[...] + jnp.where(valid_ref[...] > 0.0, acc, 0.0)


def _out_proj(h, y_conv, y_att, o_f, o_b, qkvz, dn_norm_w, valid, w_out, layer):
    n = h.shape[0]
    row = lambda width: pl.BlockSpec((TM, width), lambda i: (i, 0))
    return pl.pallas_call(
        _out_proj_body,
        grid=(n // TM,),
        in_specs=[row(D_MODEL), row(CONV_W), row(ATT_Q), row(DN_W), row(DN_W),
                  pl.BlockSpec((None, TM, DN_W), lambda i: (3, i, 0)),
                  pl.BlockSpec((None, 1, HD), lambda i: (layer, 0, 0)),
                  row(1),
                  pl.BlockSpec((None, D_MODEL, D_MODEL), lambda i: (layer, 0, 0))],
        out_specs=row(D_MODEL),
        out_shape=jax.ShapeDtypeStruct((n, D_MODEL), F32),
        compiler_params=_cparams("parallel"),
        name="out_proj",
    )(h, y_conv, y_att, o_f, o_b, qkvz, dn_norm_w, valid, w_out)


def _final_norm_body(h_ref, nw_ref, o_ref):
    x = h_ref[...]
    ms = jnp.mean(x * x, axis=-1, keepdims=True)
    o_ref[...] = x * lax.rsqrt(ms + EPS) * nw_ref[...]


def _final_norm(h, w, start_block, batch, seq):
    nb = (BLK + seq) // BLK
    return pl.pallas_call(
        _final_norm_body,
        grid=(batch, seq // BLK),
        in_specs=[pl.BlockSpec((BLK, D_MODEL), lambda b, j: (start_block + b * nb + 1 + j, 0)),
                  pl.BlockSpec((1, D_MODEL), lambda b, j: (0, 0))],
        out_specs=pl.BlockSpec((None, BLK, D_MODEL), lambda b, j: (b, j, 0)),
        out_shape=jax.ShapeDtypeStruct((batch, seq, D_MODEL), F32),
        compiler_params=_cparams("parallel", "parallel"),
        name="final_norm",
    )(h, w)


def _regroup_body(perm_ref, wt_ref, o_ref):
    o_ref[...] = wt_ref[...].T.astype(BF16)


def _regroup_w_in(w_in_t, perm):
    depth = w_in_t.shape[0]
    grid_spec = pltpu.PrefetchScalarGridSpec(
        num_scalar_prefetch=1,
        grid=(depth, N_BIG // W_COLS),
        in_specs=[pl.BlockSpec((None, W_COLS, D_MODEL), lambda l, j, perm: (l, perm[j], 0))],
        out_specs=pl.BlockSpec((None, D_MODEL, W_COLS), lambda l, j, perm: (l, 0, j)),
    )
    return pl.pallas_call(
        _regroup_body,
        grid_spec=grid_spec,
        out_shape=jax.ShapeDtypeStruct((depth, D_MODEL, N_BIG), BF16),
        compiler_params=_cparams("parallel", "parallel"),
        name="regroup_w_in",
    )(perm, w_in_t)


def _gate_weights_body(wt_ref, ws_ref, wst_ref):
    x = wt_ref[:BLK, :]
    x = jnp.where(lax.broadcasted_iota(jnp.int32, x.shape, 0) < N_GATE, x, 0.0)
    wst_ref[...] = x.astype(BF16)
    ws_ref[...] = x.T.astype(BF16)


def _gate_weights(w_in_t):
    depth = w_in_t.shape[0]
    return pl.pallas_call(
        _gate_weights_body,
        grid=(depth,),
        in_specs=[pl.BlockSpec((None, W_COLS, D_MODEL), lambda l: (l, N_BIG // W_COLS, 0))],
        out_specs=[pl.BlockSpec((None, D_MODEL, BLK), lambda l: (l, 0, 0)),
                   pl.BlockSpec((None, BLK, D_MODEL), lambda l: (l, 0, 0))],
        out_shape=[jax.ShapeDtypeStruct((depth, D_MODEL, BLK), BF16),
                   jax.ShapeDtypeStruct((depth, BLK, D_MODEL), BF16)],
        compiler_params=_cparams("parallel"),
        name="gate_weights",
    )(w_in_t)


def _tables(seq_blocks, n_blocks):
    pos, nblk, fblk, bblk, first = [], [], [], [], []
    start = 0
    for nb in seq_blocks:
        pos += list(range(nb))
        nblk += [nb] * nb
        fblk += [start + j for j in range(nb)]
        bblk += [start + nb - 1 - j for j in range(nb)]
        first += [1] + [0] * (nb - 1)
        start += nb
    pos += [0] * (n_blocks - start)
    nblk += [1] * (n_blocks - start)
    as_i32 = lambda a: jnp.asarray(np.asarray(a, np.int32))
    return as_i32(pos), as_i32(nblk), as_i32(fblk), as_i32(bblk), as_i32(first)


def _column_perm():
    c = np.cumsum([0, CONV_W, CONV_W, CONV_W, CONV_W, ATT_Q, ATT_KVW, ATT_KVW, ATT_Q,
                   DN_W, DN_W, DN_W, DN_W]) // W_COLS
    blocks = lambda a: np.arange(c[a], c[a + 1])
    cx, cb, cc, cz, aq, ak, av, az, dq, dk, dv, dz = (blocks(a) for a in range(12))
    halves = CONV_HALF // W_COLS
    conv = [g[s * halves:(s + 1) * halves] for s in range(CONV_W // CONV_HALF) for g in (cx, cb, cc, cz)]
    return np.concatenate([dq, dk, dv, dz, aq, az, ak, av] + conv).astype(np.int32)


def _forward(xs, meta_tokens, norm_w, w_in, conv_a_w, attn_sink, dn_conv_w, dn_a_log, dn_dt_bias,
             dn_norm_w, w_out, final_norm_w):
    depth = w_in.shape[0]
    lead = jnp.concatenate([jnp.zeros((PAD, D_MODEL), F32), meta_tokens.astype(F32)], axis=0)
    rows, seq_blocks = [], []
    for x in xs:
        for bi in range(x.shape[0]):
            rows += [lead, x[bi]]
            seq_blocks.append((BLK + x.shape[1]) // BLK)
    n_real = sum(seq_blocks) * BLK
    n = -(-n_real // TM_PROJ) * TM_PROJ
    rows.append(jnp.zeros((n - n_real, D_MODEL), F32))
    h = jnp.concatenate(rows, axis=0)

    valid_np = np.zeros((n, 1), np.float32)
    start = 0
    for nb in seq_blocks:
        valid_np[start + PAD:start + nb * BLK] = 1.0
        start += nb * BLK
    valid = jnp.asarray(valid_np)
    pos_tab, nblk_tab, fblk, bblk, first = _tables(seq_blocks, n // BLK)

    w_in_t = jnp.swapaxes(w_in, 1, 2)
    w_big = _regroup_w_in(w_in_t, jnp.asarray(_column_perm()))
    w_small, w_small_t = _gate_weights(w_in_t)
    pad_gate = lambda a: jnp.pad(a.reshape(depth, 2 * DN_H), ((0, 0), (2 * DN_H, BLK - N_GATE)))
    a_row = pad_gate(dn_a_log.astype(F32))[:, None, :]
    b_row = pad_gate(dn_dt_bias.astype(F32))[:, None, :]
    a_col, b_col = jnp.swapaxes(a_row, 1, 2), jnp.swapaxes(b_row, 1, 2)
    dn_conv_w4 = jnp.pad(jnp.swapaxes(dn_conv_w.reshape(depth, 3, 3, DN_W), 1, 2),
                         ((0, 0), (0, 1), (0, 0), (0, 0)))
    w_out_b = w_out.astype(BF16)
    norm_w3 = norm_w[:, None, :]
    dn_norm_w3 = dn_norm_w[:, None, :]

    for l in range(depth):
        proj_a, gates, gates_t, xn = _proj_att(h, norm_w3, w_big, w_small, w_small_t,
                                               a_row, b_row, a_col, b_col, l)
        qkvz = _proj_dn(xn, w_big, dn_conv_w4, valid, l)
        y_conv = _proj_conv(xn, w_big, conv_a_w, l)
        y_att = _attention(proj_a, attn_sink, pos_tab, nblk_tab, l)
        o_f, o_b = _dn_scan(qkvz, gates, gates_t, fblk, bblk, first)
        h = _out_proj(h, y_conv, y_att, o_f, o_b, qkvz, dn_norm_w3, valid, w_out_b, l)

    ys, start = [], 0
    for x in xs:
        b, s = x.shape[0], x.shape[1]
        ys.append(_final_norm(h, final_norm_w[None, :], start, b, s))
        start += b * (BLK + s) // BLK
    return tuple(ys)


def kernel(x_prompt, x_sample, meta_tokens, norm_w, w_in, conv_a_w, attn_sink, dn_conv_w,
           dn_a_log, dn_dt_bias, dn_norm_w, w_out, final_norm_w):
    return _forward([x_prompt, x_sample], meta_tokens, norm_w, w_in, conv_a_w, attn_sink, dn_conv_w,
                    dn_a_log, dn_dt_bias, dn_norm_w, w_out, final_norm_w)
```

```python
import numpy as np
import jax
import jax.numpy as jnp
from jax import lax
from jax.experimental import pallas as pl
from jax.experimental.pallas import tpu as pltpu

F32 = jnp.float32
BF16 = jnp.bfloat16
HIGHEST = lax.Precision.HIGHEST

D_MODEL = 2048
N_META = 16
BLK = 128
PAD = BLK - N_META
CONV_W = 512
HD = 128
ATT_HQ = 6
ATT_KV = 2
ATT_G = ATT_HQ // ATT_KV
ATT_Q = ATT_HQ * HD
ATT_KVW = ATT_KV * HD
DN_H = 6
DN_W = DN_H * HD
N_GATE = 4 * DN_H
EPS = 1e-6

N_DN = 4 * DN_W
N_ATT = 2 * ATT_Q + 2 * ATT_KVW
N_CONV = 4 * CONV_W
N_BIG = N_DN + N_ATT + N_CONV
CONV_HALF = CONV_W // 2
W_COLS = 256

TM_PROJ = 1024
TN_PROJ = 1024
ROW_CHUNK = 256
TM = 512
HALO = 8
HALO_IN = 16
VMEM_LIMIT = 56 * 1024 * 1024


def _cparams(*sem):
    return pltpu.CompilerParams(dimension_semantics=sem, vmem_limit_bytes=VMEM_LIMIT)


def _silu(x):
    return x * (1.0 / (1.0 + jnp.exp(-x)))


def _dot(a, b):
    return jnp.dot(a, b, preferred_element_type=F32)


def _dot_nt(a, b, precision=None):
    return lax.dot_general(a, b, (((1,), (1,)), ((), ())), preferred_element_type=F32,
                           precision=precision)


def _softplus(x):
    return jnp.maximum(x, 0.0) + jnp.log1p(jnp.exp(-jnp.abs(x)))


def _gate_fn(x, idx, a_log, bias):
    beta = 1.0 / (1.0 + jnp.exp(-x))
    g = -jnp.exp(a_log) * _softplus(x + bias)
    return jnp.where(idx < 2 * DN_H, beta, jnp.where(idx < N_GATE, g, 0.0))


def _halo_specs(n):
    per = TM_PROJ // HALO_IN
    last = n // HALO_IN - 1
    return [pl.BlockSpec((TM_PROJ, D_MODEL), lambda i, j: (i, 0)),
            pl.BlockSpec((HALO_IN, D_MODEL), lambda i, j: (jnp.maximum(i * per - 1, 0), 0)),
            pl.BlockSpec((HALO_IN, D_MODEL), lambda i, j: (jnp.minimum((i + 1) * per, last), 0))]


def _edge_flags():
    i = pl.program_id(0)
    return (i > 0).astype(F32), (i < pl.num_programs(0) - 1).astype(F32)


def _chunked(compute, epilogue):
    chunks = TM_PROJ // ROW_CHUNK
    for c in range(chunks):
        compute(c)
        if c > 0:
            epilogue(c - 1)
    epilogue(chunks - 1)


def _proj_main_body(h_ref, hp_ref, hn_ref, nw_ref, wdn_ref, watt_ref, ws_ref, wst_ref,
                    arow_ref, brow_ref, acol_ref, bcol_ref, cw_ref, valid_ref,
                    o_ref, pa_ref, g_ref, gt_ref, xn_ref, acc_ref, xh_ref):
    j = pl.program_id(1)

    def rmsnorm(x):
        ms = jnp.mean(x * x, axis=-1, keepdims=True)
        return (x * lax.rsqrt(ms + EPS) * nw_ref[...]).astype(BF16)

    @pl.when(j == 0)
    def _():
        for r in range(0, TM_PROJ, ROW_CHUNK):
            xn_ref[r:r + ROW_CHUNK, :] = rmsnorm(h_ref[r:r + ROW_CHUNK, :])
        xh_ref[0:HALO_IN, :] = rmsnorm(hp_ref[...])
        xh_ref[HALO_IN:, :] = rmsnorm(hn_ref[...])
        p = _dot(xn_ref[...], ws_ref[...])
        pt = _dot_nt(wst_ref[...], xn_ref[...])
        g_ref[...] = _gate_fn(p, lax.broadcasted_iota(jnp.int32, p.shape, 1), arow_ref[...], brow_ref[...])
        gt_ref[...] = _gate_fn(pt, lax.broadcasted_iota(jnp.int32, pt.shape, 0), acol_ref[...], bcol_ref[...])

    has_prev, has_next = _edge_flags()
    acc_ref[0:HALO, :] = _dot(xh_ref[0:HALO_IN, :], wdn_ref[...])[HALO_IN - HALO:, :] * has_prev
    acc_ref[HALO + TM_PROJ:, :] = _dot(xh_ref[HALO_IN:, :], wdn_ref[...])[:HALO, :] * has_next
    is_gate = j == 3
    q_scale = jnp.where(j == 0, HD ** -0.5, 1.0)

    def compute(c):
        r0 = c * ROW_CHUNK
        acc_ref[HALO + r0:HALO + r0 + ROW_CHUNK, :] = _dot(xn_ref[r0:r0 + ROW_CHUNK, :], wdn_ref[...])

    def attention_half(half):
        r0 = half * (TM_PROJ // 2)
        pa_ref[r0:r0 + TM_PROJ // 2, :] = _dot(xn_ref[r0:r0 + TM_PROJ // 2, :], watt_ref[...]).astype(BF16)

    def epilogue(c):
        r0 = c * ROW_CHUNK
        x = acc_ref[HALO + r0:HALO + r0 + ROW_CHUNK, :]
        x_m1 = acc_ref[HALO + r0 - 1:HALO + r0 - 1 + ROW_CHUNK, :]
        x_p1 = acc_ref[HALO + r0 + 1:HALO + r0 + 1 + ROW_CHUNK, :]
        conv = x_m1 * cw_ref[0:1, :] + x * cw_ref[1:2, :] + x_p1 * cw_ref[2:3, :]
        y = _silu(jnp.where(is_gate, x, conv))
        row_f = jnp.where(j == 1, valid_ref[r0:r0 + ROW_CHUNK, :], 1.0) * q_scale
        for hh in range(DN_H):
            hs = slice(hh * HD, (hh + 1) * HD)
            yh = y[:, hs]
            inv = lax.rsqrt(jnp.sum(yh * yh, axis=-1, keepdims=True) + EPS)
            o_ref[r0:r0 + ROW_CHUNK, hs] = (yh * (jnp.where(j < 2, inv, 1.0) * row_f)).astype(BF16)

    chunks = TM_PROJ // ROW_CHUNK
    for c in range(chunks):
        compute(c)
        if c > 0:
            epilogue(c - 1)
        if c == chunks // 2 - 1:
            attention_half(0)
    attention_half(1)
    epilogue(chunks - 1)


TN_ATT = N_ATT // (N_DN // DN_W)


def _proj_main(h, norm_w, w_big, w_small, w_small_t, arow, brow, acol, bcol, dn_conv_w4, valid, layer):
    n = h.shape[0]
    lay3 = lambda i, j: (layer, 0, 0)
    return pl.pallas_call(
        _proj_main_body,
        grid=(n // TM_PROJ, N_DN // DN_W),
        in_specs=_halo_specs(n) + [
            pl.BlockSpec((None, 1, D_MODEL), lay3),
            pl.BlockSpec((None, D_MODEL, DN_W), lambda i, j: (layer, 0, j)),
            pl.BlockSpec((None, D_MODEL, TN_ATT), lambda i, j: (layer, 0, N_DN // TN_ATT + j)),
            pl.BlockSpec((None, D_MODEL, BLK), lay3),
            pl.BlockSpec((None, BLK, D_MODEL), lay3),
            pl.BlockSpec((None, 1, BLK), lay3),
            pl.BlockSpec((None, 1, BLK), lay3),
            pl.BlockSpec((None, BLK, 1), lay3),
            pl.BlockSpec((None, BLK, 1), lay3),
            pl.BlockSpec((None, None, 3, DN_W), lambda i, j: (layer, j, 0, 0)),
            pl.BlockSpec((TM_PROJ, 1), lambda i, j: (i, 0)),
        ],
        out_specs=[pl.BlockSpec((None, TM_PROJ, DN_W), lambda i, j: (j, i, 0)),
                   pl.BlockSpec((TM_PROJ, TN_ATT), lambda i, j: (i, j)),
                   pl.BlockSpec((TM_PROJ, BLK), lambda i, j: (i, 0)),
                   pl.BlockSpec((BLK, TM_PROJ), lambda i, j: (0, i)),
                   pl.BlockSpec((TM_PROJ, D_MODEL), lambda i, j: (i, 0))],
        out_shape=[jax.ShapeDtypeStruct((N_DN // DN_W, n, DN_W), BF16),
                   jax.ShapeDtypeStruct((n, N_ATT), BF16),
                   jax.ShapeDtypeStruct((n, BLK), F32), jax.ShapeDtypeStruct((BLK, n), F32),
                   jax.ShapeDtypeStruct((n, D_MODEL), BF16)],
        scratch_shapes=[pltpu.VMEM((TM_PROJ + 2 * HALO, DN_W), F32),
                        pltpu.VMEM((2 * HALO_IN, D_MODEL), BF16)],
        compiler_params=_cparams("parallel", "arbitrary"),
        name="proj_main",
    )(h, h, h, norm_w, w_big, w_big, w_small, w_small_t, arow, brow, acol, bcol, dn_conv_w4, valid)


def _proj_conv_body(xn_ref, prev_ref, next_ref, w_ref, cw_ref, o_ref, u_ref, bz_ref):
    cx, cb, cc, cz = (slice(k * CONV_HALF, (k + 1) * CONV_HALF) for k in range(4))
    has_prev, has_next = _edge_flags()
    pv = _dot(prev_ref[...], w_ref[...])[HALO_IN - HALO:, :]
    nx = _dot(next_ref[...], w_ref[...])[:HALO, :]
    u_ref[0:HALO, :] = pv[:, cc] * pv[:, cx] * has_prev
    u_ref[HALO + TM_PROJ:, :] = nx[:, cc] * nx[:, cx] * has_next

    def compute(c):
        r0 = c * ROW_CHUNK
        acc = _dot(xn_ref[r0:r0 + ROW_CHUNK, :], w_ref[...])
        u_ref[HALO + r0:HALO + r0 + ROW_CHUNK, :] = acc[:, cc] * acc[:, cx]
        bz_ref[r0:r0 + ROW_CHUNK, :] = acc[:, cb] * _silu(acc[:, cz])

    def epilogue(c):
        r0 = c * ROW_CHUNK
        u = u_ref[HALO + r0:HALO + r0 + ROW_CHUNK, :]
        u_m1 = u_ref[HALO + r0 - 1:HALO + r0 - 1 + ROW_CHUNK, :]
        u_p1 = u_ref[HALO + r0 + 1:HALO + r0 + 1 + ROW_CHUNK, :]
        conv = u_m1 * cw_ref[0:1, :] + u * cw_ref[1:2, :] + u_p1 * cw_ref[2:3, :]
        o_ref[r0:r0 + ROW_CHUNK, :] = (bz_ref[r0:r0 + ROW_CHUNK, :] * conv).astype(BF16)

    _chunked(compute, epilogue)


def _proj_conv(xn, w_big, conv_w, layer):
    n = xn.shape[0]
    first = (N_DN + N_ATT) // TN_PROJ
    return pl.pallas_call(
        _proj_conv_body,
        grid=(n // TM_PROJ, N_CONV // TN_PROJ),
        in_specs=_halo_specs(n) + [
            pl.BlockSpec((None, D_MODEL, TN_PROJ), lambda i, j: (layer, 0, first + j)),
            pl.BlockSpec((None, 3, CONV_HALF), lambda i, j: (layer, 0, j)),
        ],
        out_specs=pl.BlockSpec((TM_PROJ, CONV_HALF), lambda i, j: (i, j)),
        out_shape=jax.ShapeDtypeStruct((n, CONV_W), BF16),
        scratch_shapes=[pltpu.VMEM((TM_PROJ + 2 * HALO, CONV_HALF), F32),
                        pltpu.VMEM((TM_PROJ, CONV_HALF), F32)],
        compiler_params=_cparams("parallel", "parallel"),
        name="proj_conv",
    )(xn, xn, xn, w_big, conv_w)


ATT_QB = 2
CB_AQ, CB_AZ = 0, 1
CB_AK, CB_AV = 2 * ATT_Q // ATT_KVW, 2 * ATT_Q // ATT_KVW + 1


def _attn_body(pos_ref, nblk_ref, sink_ref, q_ref, az_ref, kp, kc, kn, vp, vc, vn, o_ref):
    i = pl.program_id(0)
    qi = lax.broadcasted_iota(jnp.int32, (BLK, 3 * BLK), 0)
    kj = lax.broadcasted_iota(jnp.int32, (BLK, 3 * BLK), 1) - BLK
    dist = jnp.abs(qi - kj)
    distf = dist.astype(F32)
    allowed = []
    for j in range(ATT_QB):
        b = i * ATT_QB + j
        kabs = pos_ref[b] * BLK + kj
        allowed.append((dist <= BLK) & (kabs >= PAD) & (kabs < nblk_ref[b] * BLK))

    def band(prev_ref, cur_ref, next_ref, j, cols):
        blocks = ([prev_ref[:, cols]] + [cur_ref[r * BLK:(r + 1) * BLK, cols] for r in range(ATT_QB)]
                  + [next_ref[:, cols]])
        return jnp.concatenate(blocks[j:j + 3], axis=0)

    scores = {}
    for j in range(ATT_QB):
        rows = slice(j * BLK, (j + 1) * BLK)
        for kvh in range(ATT_KV):
            k3 = band(kp, kc, kn, j, slice(kvh * HD, (kvh + 1) * HD))
            q3 = jnp.concatenate([q_ref[rows, (kvh * ATT_G + g) * HD:(kvh * ATT_G + g + 1) * HD]
                                  for g in range(ATT_G)], axis=0)
            scores[j, kvh] = _dot_nt(q3, k3)
    probs, dens = {}, {}
    for j in range(ATT_QB):
        for head in range(ATT_HQ):
            kvh, g = divmod(head, ATT_G)
            slope = float(2.0 ** (-8.0 * (head + 1) / ATT_HQ))
            s = scores[j, kvh][g * BLK:(g + 1) * BLK] * (HD ** -0.5) - slope * distf
            s = jnp.where(allowed[j], s, -jnp.inf)
            sk = sink_ref[head]
            m = jnp.maximum(jnp.max(s, axis=-1, keepdims=True), sk)
            p = jnp.exp(s - m)
            dens[j, head] = jnp.sum(p, axis=-1, keepdims=True) + jnp.exp(sk - m)
            probs[j, head] = p.astype(BF16)
    for j in range(ATT_QB):
        rows = slice(j * BLK, (j + 1) * BLK)
        for kvh in range(ATT_KV):
            v3 = band(vp, vc, vn, j, slice(kvh * HD, (kvh + 1) * HD))
            p3 = jnp.concatenate([probs[j, kvh * ATT_G + g] for g in range(ATT_G)], axis=0)
            o3 = _dot(p3, v3)
            for g in range(ATT_G):
                head = kvh * ATT_G + g
                hs = slice(head * HD, (head + 1) * HD)
                o = o3[g * BLK:(g + 1) * BLK] / dens[j, head]
                o_ref[rows, hs] = (o * _silu(az_ref[rows, hs].astype(F32))).astype(BF16)


def _attention(proj, sink, pos_tab, nblk_tab, layer):
    n = proj.shape[0]
    nb = n // BLK
    tq = ATT_QB * BLK
    prev = lambda i, *_: (jnp.maximum(i * ATT_QB - 1, 0),)
    nxt = lambda i, *_: (jnp.minimum((i + 1) * ATT_QB, nb - 1),)
    kv_specs = []
    for c in (CB_AK, CB_AV):
        kv_specs += [pl.BlockSpec((BLK, ATT_KVW), lambda i, *_, c=c: prev(i) + (c,)),
                     pl.BlockSpec((tq, ATT_KVW), lambda i, *_, c=c: (i, c)),
                     pl.BlockSpec((BLK, ATT_KVW), lambda i, *_, c=c: nxt(i) + (c,))]
    grid_spec = pltpu.PrefetchScalarGridSpec(
        num_scalar_prefetch=2,
        grid=(nb // ATT_QB,),
        in_specs=[
            pl.BlockSpec(memory_space=pltpu.SMEM),
            pl.BlockSpec((tq, ATT_Q), lambda i, *_: (i, CB_AQ)),
            pl.BlockSpec((tq, ATT_Q), lambda i, *_: (i, CB_AZ)),
        ] + kv_specs,
        out_specs=pl.BlockSpec((tq, ATT_Q), lambda i, *_: (i, 0)),
    )
    return pl.pallas_call(
        _attn_body,
        grid_spec=grid_spec,
        out_shape=jax.ShapeDtypeStruct((n, ATT_Q), BF16),
        compiler_params=_cparams("parallel"),
        name="attention",
    )(pos_tab, nblk_tab, sink[layer], proj, proj, proj, proj, proj, proj, proj, proj)


def _dn_chunks(chains, r_i, c_i, s_ref):
    xor_idx = r_i ^ c_i
    st = []
    for q, k, v, beta, col, row, tot, lower, idx in chains:
        incl, strict = (r_i >= c_i, r_i > c_i) if lower else (r_i <= c_i, r_i < c_i)
        decay = jnp.exp(jnp.where(incl, col - row, -jnp.inf))
        kbeta = k * beta
        gram = _dot_nt(jnp.concatenate([kbeta, q], axis=0).astype(BF16), k.astype(BF16))
        p = -jnp.where(strict, gram[:BLK] * decay, 0.0)
        ecol = jnp.exp(col)
        st.append(dict(
            p=p, idx=idx, tot=tot, qk=(gram[BLK:] * decay).astype(BF16),
            rhs=jnp.concatenate([v * beta, kbeta * ecol], axis=1).astype(BF16),
            qg=(q * ecol).astype(BF16),
            k_tail_t=(k * jnp.exp(tot - col)).T.astype(BF16),
            t=(xor_idx == 0).astype(F32) + jnp.where(xor_idx == 1, p, 0.0)))
    s = 2
    while s < BLK:
        for c in st:
            c["tb"] = c["t"].astype(BF16)
            link = jnp.where((xor_idx >= s) & (xor_idx < 2 * s), c["p"], 0.0).astype(BF16)
            c["tl"] = _dot(c["tb"], link).astype(BF16)
        for c in st:
            c["t"] = c["t"] + _dot(c["tl"], c["tb"])
        s *= 2
    for c in st:
        c["x"] = _dot(c["t"].astype(BF16), c["rhs"])
    for c in st:
        c["s"] = s_ref[c["idx"]]
        c["ws"] = _dot(jnp.concatenate([c["x"][:, HD:].astype(BF16), c["qg"]], axis=0),
                       c["s"].astype(BF16))
    for c in st:
        v_new = c["x"][:, :HD] - c["ws"][:BLK]
        c["r"] = _dot(jnp.concatenate([c["qk"], c["k_tail_t"]], axis=0), v_new.astype(BF16))
    outs = []
    for c in st:
        s_ref[c["idx"]] = c["s"] * jnp.exp(c["tot"]) + c["r"][BLK:]
        outs.append(c["ws"][BLK:] + c["r"][:BLK])
    return outs


def _dn_scan_body(fblk_ref, bblk_ref, first_ref,
                  qf, kf, vf, gf, gtf, qb, kb, vb, gb, gtb, of_ref, ob_ref, s_ref):
    i = pl.program_id(0)

    @pl.when(first_ref[i] == 1)
    def _():
        s_ref[...] = jnp.zeros_like(s_ref)

    r_i = lax.broadcasted_iota(jnp.int32, (BLK, BLK), 0)
    c_i = lax.broadcasted_iota(jnp.int32, (BLK, BLK), 1)
    lower = (r_i >= c_i).astype(F32)
    upper = (r_i <= c_i).astype(F32)
    chains, dests = [], []
    for d, (q_ref, k_ref, v_ref, g_ref, gt_ref, o_ref) in enumerate(
            ((qf, kf, vf, gf, gtf, of_ref), (qb, kb, vb, gb, gtb, ob_ref))):
        if d == 0:
            csum, csum_t, last = lower, upper, BLK - 1
        else:
            csum, csum_t, last = upper, lower, 0
        gates = g_ref[...]
        gc = jnp.dot(csum, gates, preferred_element_type=F32, precision=HIGHEST)
        gct = jnp.dot(gt_ref[...], csum_t, preferred_element_type=F32, precision=HIGHEST)
        for hh in range(DN_H):
            hs = slice(hh * HD, (hh + 1) * HD)
            cb = d * DN_H + hh
            cg = 2 * DN_H + cb
            chains.append((q_ref[:, hs].astype(F32), k_ref[:, hs].astype(F32), v_ref[:, hs].astype(F32),
                           gates[:, cb:cb + 1], gc[:, cg:cg + 1], gct[cg:cg + 1, :],
                           gc[last:last + 1, cg:cg + 1], d == 0, cb))
            dests.append((o_ref, hs))
    for (o_ref, hs), o in zip(dests, _dn_chunks(chains, r_i, c_i, s_ref)):
        o_ref[:, hs] = o.astype(BF16)


def _dn_scan(qkvz, gates, gates_t, fblk, bblk, first):
    n = qkvz.shape[1]
    steps = fblk.shape[0]
    fmap = lambda i, f, b, s: (f[i], 0)
    bmap = lambda i, f, b, s: (b[i], 0)
    fmap_t = lambda i, f, b, s: (0, f[i])
    bmap_t = lambda i, f, b, s: (0, b[i])
    fwide = [pl.BlockSpec((None, BLK, DN_W), lambda i, f, b, s, a=a: (a, f[i], 0)) for a in range(3)]
    bwide = [pl.BlockSpec((None, BLK, DN_W), lambda i, f, b, s, a=a: (a, b[i], 0)) for a in range(3)]
    grid_spec = pltpu.PrefetchScalarGridSpec(
        num_scalar_prefetch=3,
        grid=(steps,),
        in_specs=fwide + [pl.BlockSpec((BLK, BLK), fmap), pl.BlockSpec((BLK, BLK), fmap_t)]
        + bwide + [pl.BlockSpec((BLK, BLK), bmap), pl.BlockSpec((BLK, BLK), bmap_t)],
        out_specs=[pl.BlockSpec((BLK, DN_W), fmap), pl.BlockSpec((BLK, DN_W), bmap)],
        scratch_shapes=[pltpu.VMEM((2 * DN_H, HD, HD), F32)],
    )
    out = jax.ShapeDtypeStruct((n, DN_W), BF16)
    return pl.pallas_call(
        _dn_scan_body,
        grid_spec=grid_spec,
        out_shape=[out, out],
        compiler_params=_cparams("arbitrary"),
        name="dn_scan",
    )(fblk, bblk, first, qkvz, qkvz, qkvz, gates, gates_t, qkvz, qkvz, qkvz, gates, gates_t)


def _out_proj_body(h_ref, yc_ref, ya_ref, of_ref, ob_ref, zs_ref, nw_ref, valid_ref, w_ref, o_ref):
    parts = []
    for hh in range(DN_H):
        hs = slice(hh * HD, (hh + 1) * HD)
        o = of_ref[:, hs].astype(F32) + ob_ref[:, hs].astype(F32)
        ms = jnp.mean(o * o, axis=-1, keepdims=True)
        on = o * lax.rsqrt(ms + EPS) * nw_ref[...]
        parts.append((on * zs_ref[:, hs].astype(F32)).astype(BF16))
    y_dn = jnp.concatenate(parts, axis=1)
    c0, c1 = CONV_W, CONV_W + ATT_Q
    acc = (_dot(yc_ref[...], w_ref[0:c0, :]) + _dot(ya_ref[...], w_ref[c0:c1, :])
           + _dot(y_dn, w_ref[c1:, :]))
    o_ref[...] = h_ref[...] + jnp.where(valid_ref[...] > 0.0, acc, 0.0)


def _out_proj(h, y_conv, y_att, o_f, o_b, qkvz, dn_norm_w, valid, w_out, layer):
    n = h.shape[0]
    row = lambda width: pl.BlockSpec((TM, width), lambda i: (i, 0))
    return pl.pallas_call(
        _out_proj_body,
        grid=(n // TM,),
        in_specs=[row(D_MODEL), row(CONV_W), row(ATT_Q), row(DN_W), row(DN_W),
                  pl.BlockSpec((None, TM, DN_W), lambda i: (3, i, 0)),
                  pl.BlockSpec((None, 1, HD), lambda i: (layer, 0, 0)),
                  row(1),
                  pl.BlockSpec((None, D_MODEL, D_MODEL), lambda i: (layer, 0, 0))],
        out_specs=row(D_MODEL),
        out_shape=jax.ShapeDtypeStruct((n, D_MODEL), F32),
        compiler_params=_cparams("parallel"),
        name="out_proj",
    )(h, y_conv, y_att, o_f, o_b, qkvz, dn_norm_w, valid, w_out)


def _final_norm_body(h_ref, nw_ref, o_ref):
    x = h_ref[...]
    ms = jnp.mean(x * x, axis=-1, keepdims=True)
    o_ref[...] = x * lax.rsqrt(ms + EPS) * nw_ref[...]


def _final_norm(h, w, start_block, batch, seq):
    nb = (BLK + seq) // BLK
    return pl.pallas_call(
        _final_norm_body,
        grid=(batch, seq // BLK),
        in_specs=[pl.BlockSpec((BLK, D_MODEL), lambda b, j: (start_block + b * nb + 1 + j, 0)),
                  pl.BlockSpec((1, D_MODEL), lambda b, j: (0, 0))],
        out_specs=pl.BlockSpec((None, BLK, D_MODEL), lambda b, j: (b, j, 0)),
        out_shape=jax.ShapeDtypeStruct((batch, seq, D_MODEL), F32),
        compiler_params=_cparams("parallel", "parallel"),
        name="final_norm",
    )(h, w)


def _regroup_body(perm_ref, wt_ref, o_ref):
    o_ref[...] = wt_ref[...].T.astype(BF16)


def _regroup_w_in(w_in_t, perm):
    depth = w_in_t.shape[0]
    grid_spec = pltpu.PrefetchScalarGridSpec(
        num_scalar_prefetch=1,
        grid=(depth, N_BIG // W_COLS),
        in_specs=[pl.BlockSpec((None, W_COLS, D_MODEL), lambda l, j, perm: (l, perm[j], 0))],
        out_specs=pl.BlockSpec((None, D_MODEL, W_COLS), lambda l, j, perm: (l, 0, j)),
    )
    return pl.pallas_call(
        _regroup_body,
        grid_spec=grid_spec,
        out_shape=jax.ShapeDtypeStruct((depth, D_MODEL, N_BIG), BF16),
        compiler_params=_cparams("parallel", "parallel"),
        name="regroup_w_in",
    )(perm, w_in_t)


def _gate_weights_body(wt_ref, ws_ref, wst_ref):
    x = wt_ref[:BLK, :]
    x = jnp.where(lax.broadcasted_iota(jnp.int32, x.shape, 0) < N_GATE, x, 0.0)
    wst_ref[...] = x.astype(BF16)
    ws_ref[...] = x.T.astype(BF16)


def _gate_weights(w_in_t):
    depth = w_in_t.shape[0]
    return pl.pallas_call(
        _gate_weights_body,
        grid=(depth,),
        in_specs=[pl.BlockSpec((None, W_COLS, D_MODEL), lambda l: (l, N_BIG // W_COLS, 0))],
        out_specs=[pl.BlockSpec((None, D_MODEL, BLK), lambda l: (l, 0, 0)),
                   pl.BlockSpec((None, BLK, D_MODEL), lambda l: (l, 0, 0))],
        out_shape=[jax.ShapeDtypeStruct((depth, D_MODEL, BLK), BF16),
                   jax.ShapeDtypeStruct((depth, BLK, D_MODEL), BF16)],
        compiler_params=_cparams("parallel"),
        name="gate_weights",
    )(w_in_t)


def _tables(seq_blocks, n_blocks):
    pos, nblk, fblk, bblk, first = [], [], [], [], []
    start = 0
    for nb in seq_blocks:
        pos += list(range(nb))
        nblk += [nb] * nb
        fblk += [start + j for j in range(nb)]
        bblk += [start + nb - 1 - j for j in range(nb)]
        first += [1] + [0] * (nb - 1)
        start += nb
    pos += [0] * (n_blocks - start)
    nblk += [1] * (n_blocks - start)
    as_i32 = lambda a: jnp.asarray(np.asarray(a, np.int32))
    return as_i32(pos), as_i32(nblk), as_i32(fblk), as_i32(bblk), as_i32(first)


def _column_perm():
    c = np.cumsum([0, CONV_W, CONV_W, CONV_W, CONV_W, ATT_Q, ATT_KVW, ATT_KVW, ATT_Q,
                   DN_W, DN_W, DN_W, DN_W]) // W_COLS
    blocks = lambda a: np.arange(c[a], c[a + 1])
    cx, cb, cc, cz, aq, ak, av, az, dq, dk, dv, dz = (blocks(a) for a in range(12))
    halves = CONV_HALF // W_COLS
    conv = [g[s * halves:(s + 1) * halves] for s in range(CONV_W // CONV_HALF) for g in (cx, cb, cc, cz)]
    return np.concatenate([dq, dk, dv, dz, aq, az, ak, av] + conv).astype(np.int32)


def _forward(xs, meta_tokens, norm_w, w_in, conv_a_w, attn_sink, dn_conv_w, dn_a_log, dn_dt_bias,
             dn_norm_w, w_out, final_norm_w):
    depth = w_in.shape[0]
    lead = jnp.concatenate([jnp.zeros((PAD, D_MODEL), F32), meta_tokens.astype(F32)], axis=0)
    rows, seq_blocks = [], []
    for x in xs:
        for bi in range(x.shape[0]):
            rows += [lead, x[bi]]
            seq_blocks.append((BLK + x.shape[1]) // BLK)
    n_real = sum(seq_blocks) * BLK
    n = -(-n_real // TM_PROJ) * TM_PROJ
    rows.append(jnp.zeros((n - n_real, D_MODEL), F32))
    h = jnp.concatenate(rows, axis=0)

    valid_np = np.zeros((n, 1), np.float32)
    start = 0
    for nb in seq_blocks:
        valid_np[start + PAD:start + nb * BLK] = 1.0
        start += nb * BLK
    valid = jnp.asarray(valid_np)
    pos_tab, nblk_tab, fblk, bblk, first = _tables(seq_blocks, n // BLK)

    w_in_t = jnp.swapaxes(w_in, 1, 2)
    w_big = _regroup_w_in(w_in_t, jnp.asarray(_column_perm()))
    w_small, w_small_t = _gate_weights(w_in_t)
    pad_gate = lambda a: jnp.pad(a.reshape(depth, 2 * DN_H), ((0, 0), (2 * DN_H, BLK - N_GATE)))
    a_row = pad_gate(dn_a_log.astype(F32))[:, None, :]
    b_row = pad_gate(dn_dt_bias.astype(F32))[:, None, :]
    a_col, b_col = jnp.swapaxes(a_row, 1, 2), jnp.swapaxes(b_row, 1, 2)
    dn_conv_w4 = jnp.pad(jnp.swapaxes(dn_conv_w.reshape(depth, 3, 3, DN_W), 1, 2),
                         ((0, 0), (0, 1), (0, 0), (0, 0)))
    w_out_b = w_out.astype(BF16)
    norm_w3 = norm_w[:, None, :]
    dn_norm_w3 = dn_norm_w[:, None, :]

    for l in range(depth):
        qkvz, proj_a, gates, gates_t, xn = _proj_main(h, norm_w3, w_big, w_small, w_small_t, a_row, b_row,
                                                      a_col, b_col, dn_conv_w4, valid, l)
        y_conv = _proj_conv(xn, w_big, conv_a_w, l)
        y_att = _attention(proj_a, attn_sink, pos_tab, nblk_tab, l)
        o_f, o_b = _dn_scan(qkvz, gates, gates_t, fblk, bblk, first)
        h = _out_proj(h, y_conv, y_att, o_f, o_b, qkvz, dn_norm_w3, valid, w_out_b, l)

    ys, start = [], 0
    for x in xs:
        b, s = x.shape[0], x.shape[1]
        ys.append(_final_norm(h, final_norm_w[None, :], start, b, s))
        start += b * (BLK + s) // BLK
    return tuple(ys)


def kernel(x_prompt, x_sample, meta_tokens, norm_w, w_in, conv_a_w, attn_sink, dn_conv_w,
           dn_a_log, dn_dt_bias, dn_norm_w, w_out, final_norm_w):
    return _forward([x_prompt, x_sample], meta_tokens, norm_w, w_in, conv_a_w, attn_sink, dn_conv_w,
                    dn_a_log, dn_dt_bias, dn_norm_w, w_out, final_norm_w)
```

```python
import numpy as np
import jax
import jax.numpy as jnp
from jax import lax
from jax.experimental import pallas as pl
from jax.experimental.pallas import tpu as pltpu

F32 = jnp.float32
BF16 = jnp.bfloat16
HIGHEST = lax.Precision.HIGHEST

D_MODEL = 2048
N_META = 16
BLK = 128
PAD = BLK - N_META
CONV_W = 512
HD = 128
ATT_HQ = 6
ATT_KV = 2
ATT_G = ATT_HQ // ATT_KV
ATT_Q = ATT_HQ * HD
ATT_KVW = ATT_KV * HD
DN_H = 6
DN_W = DN_H * HD
N_GATE = 4 * DN_H
EPS = 1e-6

N_DN = 4 * DN_W
N_ATT = 2 * ATT_Q + 2 * ATT_KVW
N_CONV = 4 * CONV_W
N_BIG = N_DN + N_ATT + N_CONV
CONV_HALF = CONV_W // 2
W_COLS = 256

TM_PROJ = 1024
TN_PROJ = 1024
ROW_CHUNK = 256
TM = 512
HALO = 8
HALO_IN = 16
VMEM_LIMIT = 56 * 1024 * 1024


def _cparams(*sem):
    return pltpu.CompilerParams(dimension_semantics=sem, vmem_limit_bytes=VMEM_LIMIT)


def _silu(x):
    return x * (1.0 / (1.0 + jnp.exp(-x)))


def _dot(a, b):
    return jnp.dot(a, b, preferred_element_type=F32)


def _dot_nt(a, b, precision=None):
    return lax.dot_general(a, b, (((1,), (1,)), ((), ())), preferred_element_type=F32,
                           precision=precision)


def _softplus(x):
    return jnp.maximum(x, 0.0) + jnp.log1p(jnp.exp(-jnp.abs(x)))


def _gate_fn(x, idx, a_log, bias):
    beta = 1.0 / (1.0 + jnp.exp(-x))
    g = -jnp.exp(a_log) * _softplus(x + bias)
    return jnp.where(idx < 2 * DN_H, beta, jnp.where(idx < N_GATE, g, 0.0))


def _halo_specs(n):
    per = TM_PROJ // HALO_IN
    last = n // HALO_IN - 1
    return [pl.BlockSpec((TM_PROJ, D_MODEL), lambda i, j: (i, 0)),
            pl.BlockSpec((HALO_IN, D_MODEL), lambda i, j: (jnp.maximum(i * per - 1, 0), 0)),
            pl.BlockSpec((HALO_IN, D_MODEL), lambda i, j: (jnp.minimum((i + 1) * per, last), 0))]


def _edge_flags():
    i = pl.program_id(0)
    return (i > 0).astype(F32), (i < pl.num_programs(0) - 1).astype(F32)


def _chunked(compute, epilogue):
    chunks = TM_PROJ // ROW_CHUNK
    for c in range(chunks):
        compute(c)
        if c > 0:
            epilogue(c - 1)
    epilogue(chunks - 1)


def _proj_main_body(h_ref, hp_ref, hn_ref, nw_ref, wdn_ref, watt_ref, wst_ref, acol_ref, bcol_ref,
                    cw_ref, valid_ref, o_ref, pa_ref, gt_ref, xn_ref, acc_ref, xh_ref):
    j = pl.program_id(1)

    def rmsnorm(x):
        ms = jnp.mean(x * x, axis=-1, keepdims=True)
        return (x * lax.rsqrt(ms + EPS) * nw_ref[...]).astype(BF16)

    @pl.when(j == 0)
    def _():
        for r in range(0, TM_PROJ, ROW_CHUNK):
            xn_ref[r:r + ROW_CHUNK, :] = rmsnorm(h_ref[r:r + ROW_CHUNK, :])
        xh_ref[0:HALO_IN, :] = rmsnorm(hp_ref[...])
        xh_ref[HALO_IN:, :] = rmsnorm(hn_ref[...])
        pt = _dot_nt(wst_ref[...], xn_ref[...])
        gt_ref[...] = _gate_fn(pt, lax.broadcasted_iota(jnp.int32, pt.shape, 0), acol_ref[...], bcol_ref[...])

    has_prev, has_next = _edge_flags()
    acc_ref[0:HALO, :] = _dot(xh_ref[0:HALO_IN, :], wdn_ref[...])[HALO_IN - HALO:, :] * has_prev
    acc_ref[HALO + TM_PROJ:, :] = _dot(xh_ref[HALO_IN:, :], wdn_ref[...])[:HALO, :] * has_next
    is_gate = j == 3
    q_scale = jnp.where(j == 0, HD ** -0.5, 1.0)

    def compute(c):
        r0 = c * ROW_CHUNK
        acc_ref[HALO + r0:HALO + r0 + ROW_CHUNK, :] = _dot(xn_ref[r0:r0 + ROW_CHUNK, :], wdn_ref[...])

    def attention_half(half):
        r0 = half * (TM_PROJ // 2)
        pa_ref[r0:r0 + TM_PROJ // 2, :] = _dot(xn_ref[r0:r0 + TM_PROJ // 2, :], watt_ref[...]).astype(BF16)

    def epilogue(c):
        r0 = c * ROW_CHUNK
        x = acc_ref[HALO + r0:HALO + r0 + ROW_CHUNK, :]
        x_m1 = acc_ref[HALO + r0 - 1:HALO + r0 - 1 + ROW_CHUNK, :]
        x_p1 = acc_ref[HALO + r0 + 1:HALO + r0 + 1 + ROW_CHUNK, :]
        conv = x_m1 * cw_ref[0:1, :] + x * cw_ref[1:2, :] + x_p1 * cw_ref[2:3, :]
        y = _silu(jnp.where(is_gate, x, conv))
        row_f = jnp.where(j == 1, valid_ref[r0:r0 + ROW_CHUNK, :], 1.0) * q_scale
        for hh in range(DN_H):
            hs = slice(hh * HD, (hh + 1) * HD)
            yh = y[:, hs]
            inv = lax.rsqrt(jnp.sum(yh * yh, axis=-1, keepdims=True) + EPS)
            o_ref[r0:r0 + ROW_CHUNK, hs] = (yh * (jnp.where(j < 2, inv, 1.0) * row_f)).astype(BF16)

    chunks = TM_PROJ // ROW_CHUNK
    for c in range(chunks):
        compute(c)
        if c > 0:
            epilogue(c - 1)
        if c == chunks // 2 - 1:
            attention_half(0)
    attention_half(1)
    epilogue(chunks - 1)


TN_ATT = N_ATT // (N_DN // DN_W)


def _proj_main(h, norm_w, w_big, w_small_t, acol, bcol, dn_conv_w4, valid, layer):
    n = h.shape[0]
    lay3 = lambda i, j: (layer, 0, 0)
    return pl.pallas_call(
        _proj_main_body,
        grid=(n // TM_PROJ, N_DN // DN_W),
        in_specs=_halo_specs(n) + [
            pl.BlockSpec((None, 1, D_MODEL), lay3),
            pl.BlockSpec((None, D_MODEL, DN_W), lambda i, j: (layer, 0, j)),
            pl.BlockSpec((None, D_MODEL, TN_ATT), lambda i, j: (layer, 0, N_DN // TN_ATT + j)),
            pl.BlockSpec((None, BLK, D_MODEL), lay3),
            pl.BlockSpec((None, BLK, 1), lay3),
            pl.BlockSpec((None, BLK, 1), lay3),
            pl.BlockSpec((None, None, 3, DN_W), lambda i, j: (layer, j, 0, 0)),
            pl.BlockSpec((TM_PROJ, 1), lambda i, j: (i, 0)),
        ],
        out_specs=[pl.BlockSpec((None, TM_PROJ, DN_W), lambda i, j: (j, i, 0)),
                   pl.BlockSpec((TM_PROJ, TN_ATT), lambda i, j: (i, j)),
                   pl.BlockSpec((BLK, TM_PROJ), lambda i, j: (0, i)),
                   pl.BlockSpec((TM_PROJ, D_MODEL), lambda i, j: (i, 0))],
        out_shape=[jax.ShapeDtypeStruct((N_DN // DN_W, n, DN_W), BF16),
                   jax.ShapeDtypeStruct((n, N_ATT), BF16),
                   jax.ShapeDtypeStruct((BLK, n), F32),
                   jax.ShapeDtypeStruct((n, D_MODEL), BF16)],
        scratch_shapes=[pltpu.VMEM((TM_PROJ + 2 * HALO, DN_W), F32),
                        pltpu.VMEM((2 * HALO_IN, D_MODEL), BF16)],
        compiler_params=_cparams("parallel", "arbitrary"),
        name="proj_main",
    )(h, h, h, norm_w, w_big, w_big, w_small_t, acol, bcol, dn_conv_w4, valid)


def _proj_conv_body(xn_ref, prev_ref, next_ref, w_ref, cw_ref, o_ref, u_ref, bz_ref):
    cx, cb, cc, cz = (slice(k * CONV_HALF, (k + 1) * CONV_HALF) for k in range(4))
    has_prev, has_next = _edge_flags()
    pv = _dot(prev_ref[...], w_ref[...])[HALO_IN - HALO:, :]
    nx = _dot(next_ref[...], w_ref[...])[:HALO, :]
    u_ref[0:HALO, :] = pv[:, cc] * pv[:, cx] * has_prev
    u_ref[HALO + TM_PROJ:, :] = nx[:, cc] * nx[:, cx] * has_next

    def compute(c):
        r0 = c * ROW_CHUNK
        acc = _dot(xn_ref[r0:r0 + ROW_CHUNK, :], w_ref[...])
        u_ref[HALO + r0:HALO + r0 + ROW_CHUNK, :] = acc[:, cc] * acc[:, cx]
        bz_ref[r0:r0 + ROW_CHUNK, :] = acc[:, cb] * _silu(acc[:, cz])

    def epilogue(c):
        r0 = c * ROW_CHUNK
        u = u_ref[HALO + r0:HALO + r0 + ROW_CHUNK, :]
        u_m1 = u_ref[HALO + r0 - 1:HALO + r0 - 1 + ROW_CHUNK, :]
        u_p1 = u_ref[HALO + r0 + 1:HALO + r0 + 1 + ROW_CHUNK, :]
        conv = u_m1 * cw_ref[0:1, :] + u * cw_ref[1:2, :] + u_p1 * cw_ref[2:3, :]
        o_ref[r0:r0 + ROW_CHUNK, :] = (bz_ref[r0:r0 + ROW_CHUNK, :] * conv).astype(BF16)

    _chunked(compute, epilogue)


def _proj_conv(xn, w_big, conv_w, layer):
    n = xn.shape[0]
    first = (N_DN + N_ATT) // TN_PROJ
    return pl.pallas_call(
        _proj_conv_body,
        grid=(n // TM_PROJ, N_CONV // TN_PROJ),
        in_specs=_halo_specs(n) + [
            pl.BlockSpec((None, D_MODEL, TN_PROJ), lambda i, j: (layer, 0, first + j)),
            pl.BlockSpec((None, 3, CONV_HALF), lambda i, j: (layer, 0, j)),
        ],
        out_specs=pl.BlockSpec((TM_PROJ, CONV_HALF), lambda i, j: (i, j)),
        out_shape=jax.ShapeDtypeStruct((n, CONV_W), BF16),
        scratch_shapes=[pltpu.VMEM((TM_PROJ + 2 * HALO, CONV_HALF), F32),
                        pltpu.VMEM((TM_PROJ, CONV_HALF), F32)],
        compiler_params=_cparams("parallel", "parallel"),
        name="proj_conv",
    )(xn, xn, xn, w_big, conv_w)


ATT_QB = 2
CB_AQ, CB_AZ = 0, 1
CB_AK, CB_AV = 2 * ATT_Q // ATT_KVW, 2 * ATT_Q // ATT_KVW + 1


def _attn_body(pos_ref, nblk_ref, sink_ref, q_ref, az_ref, kp, kc, kn, vp, vc, vn, o_ref):
    i = pl.program_id(0)
    qi = lax.broadcasted_iota(jnp.int32, (BLK, 3 * BLK), 0)
    kj = lax.broadcasted_iota(jnp.int32, (BLK, 3 * BLK), 1) - BLK
    dist = jnp.abs(qi - kj)
    distf = dist.astype(F32)
    allowed = []
    for j in range(ATT_QB):
        b = i * ATT_QB + j
        kabs = pos_ref[b] * BLK + kj
        allowed.append((dist <= BLK) & (kabs >= PAD) & (kabs < nblk_ref[b] * BLK))

    def band(prev_ref, cur_ref, next_ref, j, cols):
        blocks = ([prev_ref[:, cols]] + [cur_ref[r * BLK:(r + 1) * BLK, cols] for r in range(ATT_QB)]
                  + [next_ref[:, cols]])
        return jnp.concatenate(blocks[j:j + 3], axis=0)

    scores = {}
    for j in range(ATT_QB):
        rows = slice(j * BLK, (j + 1) * BLK)
        for kvh in range(ATT_KV):
            k3 = band(kp, kc, kn, j, slice(kvh * HD, (kvh + 1) * HD))
            q3 = jnp.concatenate([q_ref[rows, (kvh * ATT_G + g) * HD:(kvh * ATT_G + g + 1) * HD]
                                  for g in range(ATT_G)], axis=0)
            scores[j, kvh] = _dot_nt(q3, k3)
    probs, dens = {}, {}
    for j in range(ATT_QB):
        for head in range(ATT_HQ):
            kvh, g = divmod(head, ATT_G)
            slope = float(2.0 ** (-8.0 * (head + 1) / ATT_HQ))
            s = scores[j, kvh][g * BLK:(g + 1) * BLK] * (HD ** -0.5) - slope * distf
            s = jnp.where(allowed[j], s, -jnp.inf)
            sk = sink_ref[head]
            m = jnp.maximum(jnp.max(s, axis=-1, keepdims=True), sk)
            p = jnp.exp(s - m)
            dens[j, head] = jnp.sum(p, axis=-1, keepdims=True) + jnp.exp(sk - m)
            probs[j, head] = p.astype(BF16)
    for j in range(ATT_QB):
        rows = slice(j * BLK, (j + 1) * BLK)
        for kvh in range(ATT_KV):
            v3 = band(vp, vc, vn, j, slice(kvh * HD, (kvh + 1) * HD))
            p3 = jnp.concatenate([probs[j, kvh * ATT_G + g] for g in range(ATT_G)], axis=0)
            o3 = _dot(p3, v3)
            for g in range(ATT_G):
                head = kvh * ATT_G + g
                hs = slice(head * HD, (head + 1) * HD)
                o = o3[g * BLK:(g + 1) * BLK] / dens[j, head]
                o_ref[rows, hs] = (o * _silu(az_ref[rows, hs].astype(F32))).astype(BF16)


def _attention(proj, sink, pos_tab, nblk_tab, layer):
    n = proj.shape[0]
    nb = n // BLK
    tq = ATT_QB * BLK
    prev = lambda i, *_: (jnp.maximum(i * ATT_QB - 1, 0),)
    nxt = lambda i, *_: (jnp.minimum((i + 1) * ATT_QB, nb - 1),)
    kv_specs = []
    for c in (CB_AK, CB_AV):
        kv_specs += [pl.BlockSpec((BLK, ATT_KVW), lambda i, *_, c=c: prev(i) + (c,)),
                     pl.BlockSpec((tq, ATT_KVW), lambda i, *_, c=c: (i, c)),
                     pl.BlockSpec((BLK, ATT_KVW), lambda i, *_, c=c: nxt(i) + (c,))]
    grid_spec = pltpu.PrefetchScalarGridSpec(
        num_scalar_prefetch=2,
        grid=(nb // ATT_QB,),
        in_specs=[
            pl.BlockSpec(memory_space=pltpu.SMEM),
            pl.BlockSpec((tq, ATT_Q), lambda i, *_: (i, CB_AQ)),
            pl.BlockSpec((tq, ATT_Q), lambda i, *_: (i, CB_AZ)),
        ] + kv_specs,
        out_specs=pl.BlockSpec((tq, ATT_Q), lambda i, *_: (i, 0)),
    )
    return pl.pallas_call(
        _attn_body,
        grid_spec=grid_spec,
        out_shape=jax.ShapeDtypeStruct((n, ATT_Q), BF16),
        compiler_params=_cparams("parallel"),
        name="attention",
    )(pos_tab, nblk_tab, sink[layer], proj, proj, proj, proj, proj, proj, proj, proj)


def _dn_chunks(chains, r_i, c_i, s_ref):
    xor_idx = r_i ^ c_i
    st = []
    for q, k, v, beta, col, row, tot, lower, idx in chains:
        incl, strict = (r_i >= c_i, r_i > c_i) if lower else (r_i <= c_i, r_i < c_i)
        decay = jnp.exp(jnp.where(incl, col - row, -jnp.inf))
        kbeta = k * beta
        gram = _dot_nt(jnp.concatenate([kbeta, q], axis=0).astype(BF16), k.astype(BF16))
        p = -jnp.where(strict, gram[:BLK] * decay, 0.0)
        ecol = jnp.exp(col)
        st.append(dict(
            p=p, idx=idx, tot=tot, qk=(gram[BLK:] * decay).astype(BF16),
            rhs=jnp.concatenate([v * beta, kbeta * ecol], axis=1).astype(BF16),
            qg=(q * ecol).astype(BF16),
            k_tail_t=(k * jnp.exp(tot - col)).T.astype(BF16),
            t=(xor_idx == 0).astype(F32) + jnp.where(xor_idx == 1, p, 0.0)))
    s = 2
    while s < BLK:
        for c in st:
            c["tb"] = c["t"].astype(BF16)
            link = jnp.where((xor_idx >= s) & (xor_idx < 2 * s), c["p"], 0.0).astype(BF16)
            c["tl"] = _dot(c["tb"], link).astype(BF16)
        for c in st:
            c["t"] = c["t"] + _dot(c["tl"], c["tb"])
        s *= 2
    for c in st:
        c["x"] = _dot(c["t"].astype(BF16), c["rhs"])
    for c in st:
        c["s"] = s_ref[c["idx"]]
        c["ws"] = _dot(jnp.concatenate([c["x"][:, HD:].astype(BF16), c["qg"]], axis=0),
                       c["s"].astype(BF16))
    for c in st:
        v_new = c["x"][:, :HD] - c["ws"][:BLK]
        c["r"] = _dot(jnp.concatenate([c["qk"], c["k_tail_t"]], axis=0), v_new.astype(BF16))
    outs = []
    for c in st:
        s_ref[c["idx"]] = c["s"] * jnp.exp(c["tot"]) + c["r"][BLK:]
        outs.append(c["ws"][BLK:] + c["r"][:BLK])
    return outs


def _dn_scan_body(fblk_ref, bblk_ref, first_ref,
                  qf, kf, vf, gtf, qb, kb, vb, gtb, of_ref, ob_ref, s_ref):
    i = pl.program_id(0)

    @pl.when(first_ref[i] == 1)
    def _():
        s_ref[...] = jnp.zeros_like(s_ref)

    r_i = lax.broadcasted_iota(jnp.int32, (BLK, BLK), 0)
    c_i = lax.broadcasted_iota(jnp.int32, (BLK, BLK), 1)
    lower = (r_i >= c_i).astype(F32)
    upper = (r_i <= c_i).astype(F32)
    chains, dests = [], []
    for d, (q_ref, k_ref, v_ref, gt_ref, o_ref) in enumerate(
            ((qf, kf, vf, gtf, of_ref), (qb, kb, vb, gtb, ob_ref))):
        csum_t, last = (upper, BLK - 1) if d == 0 else (lower, 0)
        gates_t = gt_ref[...]
        gct = jnp.dot(gates_t, csum_t, preferred_element_type=F32, precision=HIGHEST)
        gates, gc = gates_t.T, gct.T
        for hh in range(DN_H):
            hs = slice(hh * HD, (hh + 1) * HD)
            cb = d * DN_H + hh
            cg = 2 * DN_H + cb
            chains.append((q_ref[:, hs].astype(F32), k_ref[:, hs].astype(F32), v_ref[:, hs].astype(F32),
                           gates[:, cb:cb + 1], gc[:, cg:cg + 1], gct[cg:cg + 1, :],
                           gc[last:last + 1, cg:cg + 1], d == 0, cb))
            dests.append((o_ref, hs))
    for (o_ref, hs), o in zip(dests, _dn_chunks(chains, r_i, c_i, s_ref)):
        o_ref[:, hs] = o.astype(BF16)


def _dn_scan(qkvz, gates_t, fblk, bblk, first):
    n = qkvz.shape[1]
    steps = fblk.shape[0]
    fmap = lambda i, f, b, s: (f[i], 0)
    bmap = lambda i, f, b, s: (b[i], 0)
    fmap_t = lambda i, f, b, s: (0, f[i])
    bmap_t = lambda i, f, b, s: (0, b[i])
    fwide = [pl.BlockSpec((None, BLK, DN_W), lambda i, f, b, s, a=a: (a, f[i], 0)) for a in range(3)]
    bwide = [pl.BlockSpec((None, BLK, DN_W), lambda i, f, b, s, a=a: (a, b[i], 0)) for a in range(3)]
    grid_spec = pltpu.PrefetchScalarGridSpec(
        num_scalar_prefetch=3,
        grid=(steps,),
        in_specs=fwide + [pl.BlockSpec((BLK, BLK), fmap_t)] + bwide + [pl.BlockSpec((BLK, BLK), bmap_t)],
        out_specs=[pl.BlockSpec((BLK, DN_W), fmap), pl.BlockSpec((BLK, DN_W), bmap)],
        scratch_shapes=[pltpu.VMEM((2 * DN_H, HD, HD), F32)],
    )
    out = jax.ShapeDtypeStruct((n, DN_W), BF16)
    return pl.pallas_call(
        _dn_scan_body,
        grid_spec=grid_spec,
        out_shape=[out, out],
        compiler_params=_cparams("arbitrary"),
        name="dn_scan",
    )(fblk, bblk, first, qkvz, qkvz, qkvz, gates_t, qkvz, qkvz, qkvz, gates_t)


def _out_proj_body(h_ref, yc_ref, ya_ref, of_ref, ob_ref, zs_ref, nw_ref, valid_ref, w_ref, o_ref):
    parts = []
    for hh in range(DN_H):
        hs = slice(hh * HD, (hh + 1) * HD)
        o = of_ref[:, hs].astype(F32) + ob_ref[:, hs].astype(F32)
        ms = jnp.mean(o * o, axis=-1, keepdims=True)
        on = o * lax.rsqrt(ms + EPS) * nw_ref[...]
        parts.append((on * zs_ref[:, hs].astype(F32)).astype(BF16))
    y_dn = jnp.concatenate(parts, axis=1)
    c0, c1 = CONV_W, CONV_W + ATT_Q
    acc = (_dot(yc_ref[...], w_ref[0:c0, :]) + _dot(ya_ref[...], w_ref[c0:c1, :])
           + _dot(y_dn, w_ref[c1:, :]))
    o_ref[...] = h_ref[...] + jnp.where(valid_ref[...] > 0.0, acc, 0.0)


def _out_proj(h, y_conv, y_att, o_f, o_b, qkvz, dn_norm_w, valid, w_out, layer):
    n = h.shape[0]
    row = lambda width: pl.BlockSpec((TM, width), lambda i: (i, 0))
    return pl.pallas_call(
        _out_proj_body,
        grid=(n // TM,),
        in_specs=[row(D_MODEL), row(CONV_W), row(ATT_Q), row(DN_W), row(DN_W),
                  pl.BlockSpec((None, TM, DN_W), lambda i: (3, i, 0)),
                  pl.BlockSpec((None, 1, HD), lambda i: (layer, 0, 0)),
                  row(1),
                  pl.BlockSpec((None, D_MODEL, D_MODEL), lambda i: (layer, 0, 0))],
        out_specs=row(D_MODEL),
        out_shape=jax.ShapeDtypeStruct((n, D_MODEL), F32),
        compiler_params=_cparams("parallel"),
        name="out_proj",
    )(h, y_conv, y_att, o_f, o_b, qkvz, dn_norm_w, valid, w_out)


def _final_norm_body(h_ref, nw_ref, o_ref):
    x = h_ref[...]
    ms = jnp.mean(x * x, axis=-1, keepdims=True)
    o_ref[...] = x * lax.rsqrt(ms + EPS) * nw_ref[...]


def _final_norm(h, w, start_block, batch, seq):
    nb = (BLK + seq) // BLK
    return pl.pallas_call(
        _final_norm_body,
        grid=(batch, seq // BLK),
        in_specs=[pl.BlockSpec((BLK, D_MODEL), lambda b, j: (start_block + b * nb + 1 + j, 0)),
                  pl.BlockSpec((1, D_MODEL), lambda b, j: (0, 0))],
        out_specs=pl.BlockSpec((None, BLK, D_MODEL), lambda b, j: (b, j, 0)),
        out_shape=jax.ShapeDtypeStruct((batch, seq, D_MODEL), F32),
        compiler_params=_cparams("parallel", "parallel"),
        name="final_norm",
    )(h, w)


def _regroup_body(perm_ref, wt_ref, o_ref):
    o_ref[...] = wt_ref[...].T.astype(BF16)


def _regroup_w_in(w_in_t, perm):
    depth = w_in_t.shape[0]
    grid_spec = pltpu.PrefetchScalarGridSpec(
        num_scalar_prefetch=1,
        grid=(depth, N_BIG // W_COLS),
        in_specs=[pl.BlockSpec((None, W_COLS, D_MODEL), lambda l, j, perm: (l, perm[j], 0))],
        out_specs=pl.BlockSpec((None, D_MODEL, W_COLS), lambda l, j, perm: (l, 0, j)),
    )
    return pl.pallas_call(
        _regroup_body,
        grid_spec=grid_spec,
        out_shape=jax.ShapeDtypeStruct((depth, D_MODEL, N_BIG), BF16),
        compiler_params=_cparams("parallel", "parallel"),
        name="regroup_w_in",
    )(perm, w_in_t)


def _gate_weights_body(wt_ref, wst_ref):
    x = wt_ref[:BLK, :]
    wst_ref[...] = jnp.where(lax.broadcasted_iota(jnp.int32, x.shape, 0) < N_GATE, x, 0.0).astype(BF16)


def _gate_weights(w_in_t):
    depth = w_in_t.shape[0]
    return pl.pallas_call(
        _gate_weights_body,
        grid=(depth,),
        in_specs=[pl.BlockSpec((None, W_COLS, D_MODEL), lambda l: (l, N_BIG // W_COLS, 0))],
        out_specs=pl.BlockSpec((None, BLK, D_MODEL), lambda l: (l, 0, 0)),
        out_shape=jax.ShapeDtypeStruct((depth, BLK, D_MODEL), BF16),
        compiler_params=_cparams("parallel"),
        name="gate_weights",
    )(w_in_t)


def _tables(seq_blocks, n_blocks):
    pos, nblk, fblk, bblk, first = [], [], [], [], []
    start = 0
    for nb in seq_blocks:
        pos += list(range(nb))
        nblk += [nb] * nb
        fblk += [start + j for j in range(nb)]
        bblk += [start + nb - 1 - j for j in range(nb)]
        first += [1] + [0] * (nb - 1)
        start += nb
    pos += [0] * (n_blocks - start)
    nblk += [1] * (n_blocks - start)
    as_i32 = lambda a: jnp.asarray(np.asarray(a, np.int32))
    return as_i32(pos), as_i32(nblk), as_i32(fblk), as_i32(bblk), as_i32(first)


def _column_perm():
    c = np.cumsum([0, CONV_W, CONV_W, CONV_W, CONV_W, ATT_Q, ATT_KVW, ATT_KVW, ATT_Q,
                   DN_W, DN_W, DN_W, DN_W]) // W_COLS
    blocks = lambda a: np.arange(c[a], c[a + 1])
    cx, cb, cc, cz, aq, ak, av, az, dq, dk, dv, dz = (blocks(a) for a in range(12))
    halves = CONV_HALF // W_COLS
    conv = [g[s * halves:(s + 1) * halves] for s in range(CONV_W // CONV_HALF) for g in (cx, cb, cc, cz)]
    return np.concatenate([dq, dk, dv, dz, aq, az, ak, av] + conv).astype(np.int32)


def _forward(xs, meta_tokens, norm_w, w_in, conv_a_w, attn_sink, dn_conv_w, dn_a_log, dn_dt_bias,
             dn_norm_w, w_out, final_norm_w):
    depth = w_in.shape[0]
    lead = jnp.concatenate([jnp.zeros((PAD, D_MODEL), F32), meta_tokens.astype(F32)], axis=0)
    rows, seq_blocks = [], []
    for x in xs:
        for bi in range(x.shape[0]):
            rows += [lead, x[bi]]
            seq_blocks.append((BLK + x.shape[1]) // BLK)
    n_real = sum(seq_blocks) * BLK
    n = -(-n_real // TM_PROJ) * TM_PROJ
    rows.append(jnp.zeros((n - n_real, D_MODEL), F32))
    h = jnp.concatenate(rows, axis=0)

    valid_np = np.zeros((n, 1), np.float32)
    start = 0
    for nb in seq_blocks:
        valid_np[start + PAD:start + nb * BLK] = 1.0
        start += nb * BLK
    valid = jnp.asarray(valid_np)
    pos_tab, nblk_tab, fblk, bblk, first = _tables(seq_blocks, n // BLK)

    w_in_t = jnp.swapaxes(w_in, 1, 2)
    w_big = _regroup_w_in(w_in_t, jnp.asarray(_column_perm()))
    w_small_t = _gate_weights(w_in_t)
    pad_gate = lambda a: jnp.pad(a.reshape(depth, 2 * DN_H), ((0, 0), (2 * DN_H, BLK - N_GATE)))[:, :, None]
    a_col, b_col = pad_gate(dn_a_log.astype(F32)), pad_gate(dn_dt_bias.astype(F32))
    dn_conv_w4 = jnp.pad(jnp.swapaxes(dn_conv_w.reshape(depth, 3, 3, DN_W), 1, 2),
                         ((0, 0), (0, 1), (0, 0), (0, 0)))
    w_out_b = w_out.astype(BF16)
    norm_w3 = norm_w[:, None, :]
    dn_norm_w3 = dn_norm_w[:, None, :]

    for l in range(depth):
        qkvz, proj_a, gates_t, xn = _proj_main(h, norm_w3, w_big, w_small_t, a_col, b_col,
                                               dn_conv_w4, valid, l)
        y_conv = _proj_conv(xn, w_big, conv_a_w, l)
        y_att = _attention(proj_a, attn_sink, pos_tab, nblk_tab, l)
        o_f, o_b = _dn_scan(qkvz, gates_t, fblk, bblk, first)
        h = _out_proj(h, y_conv, y_att, o_f, o_b, qkvz, dn_norm_w3, valid, w_out_b, l)

    ys, start = [], 0
    for x in xs:
        b, s = x.shape[0], x.shape[1]
        ys.append(_final_norm(h, final_norm_w[None, :], start, b, s))
        start += b * (BLK + s) // BLK
    return tuple(ys)


def kernel(x_prompt, x_sample, meta_tokens, norm_w, w_in, conv_a_w, attn_sink, dn_conv_w,
           dn_a_log, dn_dt_bias, dn_norm_w, w_out, final_norm_w):
    return _forward([x_prompt, x_sample], meta_tokens, norm_w, w_in, conv_a_w, attn_sink, dn_conv_w,
                    dn_a_log, dn_dt_bias, dn_norm_w, w_out, final_norm_w)
```

```python
import numpy as np
import jax
import jax.numpy as jnp
from jax import lax
from jax.experimental import pallas as pl
from jax.experimental.pallas import tpu as pltpu

F32 = jnp.float32
BF16 = jnp.bfloat16
HIGHEST = lax.Precision.HIGHEST

D_MODEL = 2048
N_META = 16
BLK = 128
PAD = BLK - N_META
CONV_W = 512
HD = 128
ATT_HQ = 6
ATT_KV = 2
ATT_G = ATT_HQ // ATT_KV
ATT_Q = ATT_HQ * HD
ATT_KVW = ATT_KV * HD
DN_H = 6
DN_W = DN_H * HD
N_GATE = 4 * DN_H
EPS = 1e-6

N_DN = 4 * DN_W
N_ATT = 2 * ATT_Q + 2 * ATT_KVW
N_CONV = 4 * CONV_W
N_BIG = N_DN + N_ATT + N_CONV
CONV_HALF = CONV_W // 2
W_COLS = 256

TM_PROJ = 1024
TN_PROJ = 1024
ROW_CHUNK = 256
TM = 512
HALO = 8
HALO_IN = 16
VMEM_LIMIT = 56 * 1024 * 1024


def _cparams(*sem):
    return pltpu.CompilerParams(dimension_semantics=sem, vmem_limit_bytes=VMEM_LIMIT)


def _silu(x):
    return x * (1.0 / (1.0 + jnp.exp(-x)))


def _dot(a, b):
    return jnp.dot(a, b, preferred_element_type=F32)


def _dot_nt(a, b, precision=None):
    return lax.dot_general(a, b, (((1,), (1,)), ((), ())), preferred_element_type=F32,
                           precision=precision)


def _softplus(x):
    return jnp.maximum(x, 0.0) + jnp.log1p(jnp.exp(-jnp.abs(x)))


def _gate_fn(x, idx, a_log, bias):
    beta = 1.0 / (1.0 + jnp.exp(-x))
    g = -jnp.exp(a_log) * _softplus(x + bias)
    return jnp.where(idx < 2 * DN_H, beta, jnp.where(idx < N_GATE, g, 0.0))


def _halo_specs(n):
    per = TM_PROJ // HALO_IN
    last = n // HALO_IN - 1
    return [pl.BlockSpec((TM_PROJ, D_MODEL), lambda i, j: (i, 0)),
            pl.BlockSpec((HALO_IN, D_MODEL), lambda i, j: (jnp.maximum(i * per - 1, 0), 0)),
            pl.BlockSpec((HALO_IN, D_MODEL), lambda i, j: (jnp.minimum((i + 1) * per, last), 0))]


def _edge_flags():
    i = pl.program_id(0)
    return (i > 0).astype(F32), (i < pl.num_programs(0) - 1).astype(F32)


def _chunked(compute, epilogue):
    chunks = TM_PROJ // ROW_CHUNK
    for c in range(chunks):
        compute(c)
        if c > 0:
            epilogue(c - 1)
    epilogue(chunks - 1)


def _proj_main_body(h_ref, hp_ref, hn_ref, nw_ref, wdn_ref, watt_ref, wst_ref, acol_ref, bcol_ref,
                    cw_ref, valid_ref, o_ref, pa_ref, gt_ref, xn_ref, acc_ref, xh_ref):
    j = pl.program_id(1)

    def rmsnorm(x):
        ms = jnp.mean(x * x, axis=-1, keepdims=True)
        return (x * lax.rsqrt(ms + EPS) * nw_ref[...]).astype(BF16)

    @pl.when(j == 0)
    def _():
        for r in range(0, TM_PROJ, ROW_CHUNK):
            xn_ref[r:r + ROW_CHUNK, :] = rmsnorm(h_ref[r:r + ROW_CHUNK, :])
        xh_ref[0:HALO_IN, :] = rmsnorm(hp_ref[...])
        xh_ref[HALO_IN:, :] = rmsnorm(hn_ref[...])
        pt = _dot_nt(wst_ref[...], xn_ref[...])
        gt_ref[...] = _gate_fn(pt, lax.broadcasted_iota(jnp.int32, pt.shape, 0), acol_ref[...], bcol_ref[...])

    has_prev, has_next = _edge_flags()
    acc_ref[0:HALO, :] = _dot(xh_ref[0:HALO_IN, :], wdn_ref[...])[HALO_IN - HALO:, :] * has_prev
    acc_ref[HALO + TM_PROJ:, :] = _dot(xh_ref[HALO_IN:, :], wdn_ref[...])[:HALO, :] * has_next
    is_gate = j == 3
    q_scale = jnp.where(j == 0, HD ** -0.5, 1.0)

    def compute(c):
        r0 = c * ROW_CHUNK
        acc_ref[HALO + r0:HALO + r0 + ROW_CHUNK, :] = _dot(xn_ref[r0:r0 + ROW_CHUNK, :], wdn_ref[...])

    def attention_half(half):
        r0 = half * (TM_PROJ // 2)
        pa_ref[r0:r0 + TM_PROJ // 2, :] = _dot(xn_ref[r0:r0 + TM_PROJ // 2, :], watt_ref[...]).astype(BF16)

    def epilogue(c):
        r0 = c * ROW_CHUNK
        x = acc_ref[HALO + r0:HALO + r0 + ROW_CHUNK, :]
        x_m1 = acc_ref[HALO + r0 - 1:HALO + r0 - 1 + ROW_CHUNK, :]
        x_p1 = acc_ref[HALO + r0 + 1:HALO + r0 + 1 + ROW_CHUNK, :]
        conv = x_m1 * cw_ref[0:1, :] + x * cw_ref[1:2, :] + x_p1 * cw_ref[2:3, :]
        y = _silu(jnp.where(is_gate, x, conv))
        row_f = jnp.where(j == 1, valid_ref[r0:r0 + ROW_CHUNK, :], 1.0) * q_scale
        for hh in range(DN_H):
            hs = slice(hh * HD, (hh + 1) * HD)
            yh = y[:, hs]
            inv = lax.rsqrt(jnp.sum(yh * yh, axis=-1, keepdims=True) + EPS)
            o_ref[r0:r0 + ROW_CHUNK, hs] = (yh * (jnp.where(j < 2, inv, 1.0) * row_f)).astype(BF16)

    chunks = TM_PROJ // ROW_CHUNK
    for c in range(chunks):
        compute(c)
        if c > 0:
            epilogue(c - 1)
        if c == chunks // 2 - 1:
            attention_half(0)
    attention_half(1)
    epilogue(chunks - 1)


TN_ATT = N_ATT // (N_DN // DN_W)


def _proj_main(h, norm_w, w_big, w_small_t, acol, bcol, dn_conv_w4, valid, layer):
    n = h.shape[0]
    lay3 = lambda i, j: (layer, 0, 0)
    return pl.pallas_call(
        _proj_main_body,
        grid=(n // TM_PROJ, N_DN // DN_W),
        in_specs=_halo_specs(n) + [
            pl.BlockSpec((None, 1, D_MODEL), lay3),
            pl.BlockSpec((None, D_MODEL, DN_W), lambda i, j: (layer, 0, j)),
            pl.BlockSpec((None, D_MODEL, TN_ATT), lambda i, j: (layer, 0, N_DN // TN_ATT + j)),
            pl.BlockSpec((None, BLK, D_MODEL), lay3),
            pl.BlockSpec((None, BLK, 1), lay3),
            pl.BlockSpec((None, BLK, 1), lay3),
            pl.BlockSpec((None, None, 3, DN_W), lambda i, j: (layer, j, 0, 0)),
            pl.BlockSpec((TM_PROJ, 1), lambda i, j: (i, 0)),
        ],
        out_specs=[pl.BlockSpec((None, TM_PROJ, DN_W), lambda i, j: (j, i, 0)),
                   pl.BlockSpec((TM_PROJ, TN_ATT), lambda i, j: (i, j)),
                   pl.BlockSpec((BLK, TM_PROJ), lambda i, j: (0, i)),
                   pl.BlockSpec((TM_PROJ, D_MODEL), lambda i, j: (i, 0))],
        out_shape=[jax.ShapeDtypeStruct((N_DN // DN_W, n, DN_W), BF16),
                   jax.ShapeDtypeStruct((n, N_ATT), BF16),
                   jax.ShapeDtypeStruct((BLK, n), F32),
                   jax.ShapeDtypeStruct((n, D_MODEL), BF16)],
        scratch_shapes=[pltpu.VMEM((TM_PROJ + 2 * HALO, DN_W), F32),
                        pltpu.VMEM((2 * HALO_IN, D_MODEL), BF16)],
        compiler_params=_cparams("parallel", "arbitrary"),
        name="proj_main",
    )(h, h, h, norm_w, w_big, w_big, w_small_t, acol, bcol, dn_conv_w4, valid)


def _proj_conv_body(xn_ref, prev_ref, next_ref, w_ref, cw_ref, o_ref, u_ref, bz_ref):
    cx, cb, cc, cz = (slice(k * CONV_HALF, (k + 1) * CONV_HALF) for k in range(4))
    has_prev, has_next = _edge_flags()
    pv = _dot(prev_ref[...], w_ref[...])[HALO_IN - HALO:, :]
    nx = _dot(next_ref[...], w_ref[...])[:HALO, :]
    u_ref[0:HALO, :] = pv[:, cc] * pv[:, cx] * has_prev
    u_ref[HALO + TM_PROJ:, :] = nx[:, cc] * nx[:, cx] * has_next

    def compute(c):
        r0 = c * ROW_CHUNK
        acc = _dot(xn_ref[r0:r0 + ROW_CHUNK, :], w_ref[...])
        u_ref[HALO + r0:HALO + r0 + ROW_CHUNK, :] = acc[:, cc] * acc[:, cx]
        bz_ref[r0:r0 + ROW_CHUNK, :] = acc[:, cb] * _silu(acc[:, cz])

    def epilogue(c):
        r0 = c * ROW_CHUNK
        u = u_ref[HALO + r0:HALO + r0 + ROW_CHUNK, :]
        u_m1 = u_ref[HALO + r0 - 1:HALO + r0 - 1 + ROW_CHUNK, :]
        u_p1 = u_ref[HALO + r0 + 1:HALO + r0 + 1 + ROW_CHUNK, :]
        conv = u_m1 * cw_ref[0:1, :] + u * cw_ref[1:2, :] + u_p1 * cw_ref[2:3, :]
        o_ref[r0:r0 + ROW_CHUNK, :] = (bz_ref[r0:r0 + ROW_CHUNK, :] * conv).astype(BF16)

    _chunked(compute, epilogue)


def _proj_conv(xn, w_big, conv_w, layer):
    n = xn.shape[0]
    first = (N_DN + N_ATT) // TN_PROJ
    return pl.pallas_call(
        _proj_conv_body,
        grid=(n // TM_PROJ, N_CONV // TN_PROJ),
        in_specs=_halo_specs(n) + [
            pl.BlockSpec((None, D_MODEL, TN_PROJ), lambda i, j: (layer, 0, first + j)),
            pl.BlockSpec((None, 3, CONV_HALF), lambda i, j: (layer, 0, j)),
        ],
        out_specs=pl.BlockSpec((TM_PROJ, CONV_HALF), lambda i, j: (i, j)),
        out_shape=jax.ShapeDtypeStruct((n, CONV_W), BF16),
        scratch_shapes=[pltpu.VMEM((TM_PROJ + 2 * HALO, CONV_HALF), F32),
                        pltpu.VMEM((TM_PROJ, CONV_HALF), F32)],
        compiler_params=_cparams("parallel", "parallel"),
        name="proj_conv",
    )(xn, xn, xn, w_big, conv_w)


ATT_QB = 4
CB_AQ, CB_AZ = 0, 1
CB_AK, CB_AV = 2 * ATT_Q // ATT_KVW, 2 * ATT_Q // ATT_KVW + 1


def _attn_body(pos_ref, nblk_ref, sink_ref, q_ref, az_ref, kp, kc, kn, vp, vc, vn, o_ref):
    i = pl.program_id(0)
    qi = lax.broadcasted_iota(jnp.int32, (BLK, 3 * BLK), 0)
    kj = lax.broadcasted_iota(jnp.int32, (BLK, 3 * BLK), 1) - BLK
    dist = jnp.abs(qi - kj)
    distf = dist.astype(F32)
    allowed = []
    for j in range(ATT_QB):
        b = i * ATT_QB + j
        kabs = pos_ref[b] * BLK + kj
        allowed.append((dist <= BLK) & (kabs >= PAD) & (kabs < nblk_ref[b] * BLK))

    def band(prev_ref, cur_ref, next_ref, j, cols):
        blocks = ([prev_ref[:, cols]] + [cur_ref[r * BLK:(r + 1) * BLK, cols] for r in range(ATT_QB)]
                  + [next_ref[:, cols]])
        return jnp.concatenate(blocks[j:j + 3], axis=0)

    scores = {}
    for j in range(ATT_QB):
        rows = slice(j * BLK, (j + 1) * BLK)
        for kvh in range(ATT_KV):
            k3 = band(kp, kc, kn, j, slice(kvh * HD, (kvh + 1) * HD))
            q3 = jnp.concatenate([q_ref[rows, (kvh * ATT_G + g) * HD:(kvh * ATT_G + g + 1) * HD]
                                  for g in range(ATT_G)], axis=0)
            scores[j, kvh] = _dot_nt(q3, k3)
    probs, dens = {}, {}
    for j in range(ATT_QB):
        for head in range(ATT_HQ):
            kvh, g = divmod(head, ATT_G)
            slope = float(2.0 ** (-8.0 * (head + 1) / ATT_HQ))
            s = scores[j, kvh][g * BLK:(g + 1) * BLK] * (HD ** -0.5) - slope * distf
            s = jnp.where(allowed[j], s, -jnp.inf)
            sk = sink_ref[head]
            m = jnp.maximum(jnp.max(s, axis=-1, keepdims=True), sk)
            p = jnp.exp(s - m)
            dens[j, head] = jnp.sum(p, axis=-1, keepdims=True) + jnp.exp(sk - m)
            probs[j, head] = p.astype(BF16)
    for j in range(ATT_QB):
        rows = slice(j * BLK, (j + 1) * BLK)
        for kvh in range(ATT_KV):
            v3 = band(vp, vc, vn, j, slice(kvh * HD, (kvh + 1) * HD))
            p3 = jnp.concatenate([probs[j, kvh * ATT_G + g] for g in range(ATT_G)], axis=0)
            o3 = _dot(p3, v3)
            for g in range(ATT_G):
                head = kvh * ATT_G + g
                hs = slice(head * HD, (head + 1) * HD)
                o = o3[g * BLK:(g + 1) * BLK] / dens[j, head]
                o_ref[rows, hs] = (o * _silu(az_ref[rows, hs].astype(F32))).astype(BF16)


def _attention(proj, sink, pos_tab, nblk_tab, layer):
    n = proj.shape[0]
    nb = n // BLK
    tq = ATT_QB * BLK
    prev = lambda i, *_: (jnp.maximum(i * ATT_QB - 1, 0),)
    nxt = lambda i, *_: (jnp.minimum((i + 1) * ATT_QB, nb - 1),)
    kv_specs = []
    for c in (CB_AK, CB_AV):
        kv_specs += [pl.BlockSpec((BLK, ATT_KVW), lambda i, *_, c=c: prev(i) + (c,)),
                     pl.BlockSpec((tq, ATT_KVW), lambda i, *_, c=c: (i, c)),
                     pl.BlockSpec((BLK, ATT_KVW), lambda i, *_, c=c: nxt(i) + (c,))]
    grid_spec = pltpu.PrefetchScalarGridSpec(
        num_scalar_prefetch=2,
        grid=(nb // ATT_QB,),
        in_specs=[
            pl.BlockSpec(memory_space=pltpu.SMEM),
            pl.BlockSpec((tq, ATT_Q), lambda i, *_: (i, CB_AQ)),
            pl.BlockSpec((tq, ATT_Q), lambda i, *_: (i, CB_AZ)),
        ] + kv_specs,
        out_specs=pl.BlockSpec((tq, ATT_Q), lambda i, *_: (i, 0)),
    )
    return pl.pallas_call(
        _attn_body,
        grid_spec=grid_spec,
        out_shape=jax.ShapeDtypeStruct((n, ATT_Q), BF16),
        compiler_params=_cparams("parallel"),
        name="attention",
    )(pos_tab, nblk_tab, sink[layer], proj, proj, proj, proj, proj, proj, proj, proj)


def _dn_chunks(chains, r_i, c_i, s_ref):
    xor_idx = r_i ^ c_i
    st = []
    for q, k, v, beta, col, row, tot, lower, idx in chains:
        incl, strict = (r_i >= c_i, r_i > c_i) if lower else (r_i <= c_i, r_i < c_i)
        decay = jnp.exp(jnp.where(incl, col - row, -jnp.inf))
        kbeta = k * beta
        gram = _dot_nt(jnp.concatenate([kbeta, q], axis=0).astype(BF16), k.astype(BF16))
        p = -jnp.where(strict, gram[:BLK] * decay, 0.0)
        ecol = jnp.exp(col)
        st.append(dict(
            p=p, idx=idx, tot=tot, qk=(gram[BLK:] * decay).astype(BF16),
            rhs=jnp.concatenate([v * beta, kbeta * ecol], axis=1).astype(BF16),
            qg=(q * ecol).astype(BF16),
            k_tail_t=(k * jnp.exp(tot - col)).T.astype(BF16),
            t=(xor_idx == 0).astype(F32) + jnp.where(xor_idx == 1, p, 0.0)))
    s = 2
    while s < BLK:
        for c in st:
            c["tb"] = c["t"].astype(BF16)
            link = jnp.where((xor_idx >= s) & (xor_idx < 2 * s), c["p"], 0.0).astype(BF16)
            c["tl"] = _dot(c["tb"], link).astype(BF16)
        for c in st:
            c["t"] = c["t"] + _dot(c["tl"], c["tb"])
        s *= 2
    for c in st:
        c["x"] = _dot(c["t"].astype(BF16), c["rhs"])
    for c in st:
        c["s"] = s_ref[c["idx"]]
        c["ws"] = _dot(jnp.concatenate([c["x"][:, HD:].astype(BF16), c["qg"]], axis=0),
                       c["s"].astype(BF16))
    for c in st:
        v_new = c["x"][:, :HD] - c["ws"][:BLK]
        c["r"] = _dot(jnp.concatenate([c["qk"], c["k_tail_t"]], axis=0), v_new.astype(BF16))
    outs = []
    for c in st:
        s_ref[c["idx"]] = c["s"] * jnp.exp(c["tot"]) + c["r"][BLK:]
        outs.append(c["ws"][BLK:] + c["r"][:BLK])
    return outs


def _dn_scan_body(fblk_ref, bblk_ref, first_ref,
                  qf, kf, vf, gtf, qb, kb, vb, gtb, of_ref, ob_ref, s_ref):
    i = pl.program_id(0)

    @pl.when(first_ref[i] == 1)
    def _():
        s_ref[...] = jnp.zeros_like(s_ref)

    r_i = lax.broadcasted_iota(jnp.int32, (BLK, BLK), 0)
    c_i = lax.broadcasted_iota(jnp.int32, (BLK, BLK), 1)
    lower = (r_i >= c_i).astype(F32)
    upper = (r_i <= c_i).astype(F32)
    chains, dests = [], []
    for d, (q_ref, k_ref, v_ref, gt_ref, o_ref) in enumerate(
            ((qf, kf, vf, gtf, of_ref), (qb, kb, vb, gtb, ob_ref))):
        csum_t, last = (upper, BLK - 1) if d == 0 else (lower, 0)
        gates_t = gt_ref[...]
        gct = jnp.dot(gates_t, csum_t, preferred_element_type=F32, precision=HIGHEST)
        gates, gc = gates_t.T, gct.T
        for hh in range(DN_H):
            hs = slice(hh * HD, (hh + 1) * HD)
            cb = d * DN_H + hh
            cg = 2 * DN_H + cb
            chains.append((q_ref[:, hs].astype(F32), k_ref[:, hs].astype(F32), v_ref[:, hs].astype(F32),
                           gates[:, cb:cb + 1], gc[:, cg:cg + 1], gct[cg:cg + 1, :],
                           gc[last:last + 1, cg:cg + 1], d == 0, cb))
            dests.append((o_ref, hs))
    for (o_ref, hs), o in zip(dests, _dn_chunks(chains, r_i, c_i, s_ref)):
        o_ref[:, hs] = o.astype(BF16)


def _dn_scan(qkvz, gates_t, fblk, bblk, first):
    n = qkvz.shape[1]
    steps = fblk.shape[0]
    fmap = lambda i, f, b, s: (f[i], 0)
    bmap = lambda i, f, b, s: (b[i], 0)
    fmap_t = lambda i, f, b, s: (0, f[i])
    bmap_t = lambda i, f, b, s: (0, b[i])
    fwide = [pl.BlockSpec((None, BLK, DN_W), lambda i, f, b, s, a=a: (a, f[i], 0)) for a in range(3)]
    bwide = [pl.BlockSpec((None, BLK, DN_W), lambda i, f, b, s, a=a: (a, b[i], 0)) for a in range(3)]
    grid_spec = pltpu.PrefetchScalarGridSpec(
        num_scalar_prefetch=3,
        grid=(steps,),
        in_specs=fwide + [pl.BlockSpec((BLK, BLK), fmap_t)] + bwide + [pl.BlockSpec((BLK, BLK), bmap_t)],
        out_specs=[pl.BlockSpec((BLK, DN_W), fmap), pl.BlockSpec((BLK, DN_W), bmap)],
        scratch_shapes=[pltpu.VMEM((2 * DN_H, HD, HD), F32)],
    )
    out = jax.ShapeDtypeStruct((n, DN_W), BF16)
    return pl.pallas_call(
        _dn_scan_body,
        grid_spec=grid_spec,
        out_shape=[out, out],
        compiler_params=_cparams("arbitrary"),
        name="dn_scan",
    )(fblk, bblk, first, qkvz, qkvz, qkvz, gates_t, qkvz, qkvz, qkvz, gates_t)


def _out_proj_body(h_ref, yc_ref, ya_ref, of_ref, ob_ref, zs_ref, nw_ref, valid_ref, w_ref, o_ref):
    parts = []
    for hh in range(DN_H):
        hs = slice(hh * HD, (hh + 1) * HD)
        o = of_ref[:, hs].astype(F32) + ob_ref[:, hs].astype(F32)
        ms = jnp.mean(o * o, axis=-1, keepdims=True)
        on = o * lax.rsqrt(ms + EPS) * nw_ref[...]
        parts.append((on * zs_ref[:, hs].astype(F32)).astype(BF16))
    y_dn = jnp.concatenate(parts, axis=1)
    c0, c1 = CONV_W, CONV_W + ATT_Q
    acc = (_dot(yc_ref[...], w_ref[0:c0, :]) + _dot(ya_ref[...], w_ref[c0:c1, :])
           + _dot(y_dn, w_ref[c1:, :]))
    o_ref[...] = h_ref[...] + jnp.where(valid_ref[...] > 0.0, acc, 0.0)


def _out_proj(h, y_conv, y_att, o_f, o_b, qkvz, dn_norm_w, valid, w_out, layer):
    n = h.shape[0]
    row = lambda width: pl.BlockSpec((TM, width), lambda i: (i, 0))
    return pl.pallas_call(
        _out_proj_body,
        grid=(n // TM,),
        in_specs=[row(D_MODEL), row(CONV_W), row(ATT_Q), row(DN_W), row(DN_W),
                  pl.BlockSpec((None, TM, DN_W), lambda i: (3, i, 0)),
                  pl.BlockSpec((None, 1, HD), lambda i: (layer, 0, 0)),
                  row(1),
                  pl.BlockSpec((None, D_MODEL, D_MODEL), lambda i: (layer, 0, 0))],
        out_specs=row(D_MODEL),
        out_shape=jax.ShapeDtypeStruct((n, D_MODEL), F32),
        compiler_params=_cparams("parallel"),
        name="out_proj",
    )(h, y_conv, y_att, o_f, o_b, qkvz, dn_norm_w, valid, w_out)


FN_BLOCKS = 4


def _final_norm_body(*refs):
    *h_refs, nw_ref, o_ref = refs
    for r, h_ref in enumerate(h_refs):
        x = h_ref[...]
        ms = jnp.mean(x * x, axis=-1, keepdims=True)
        o_ref[r * BLK:(r + 1) * BLK, :] = x * lax.rsqrt(ms + EPS) * nw_ref[...]


def _final_norm(h, w, start_block, batch, seq):
    nb = (BLK + seq) // BLK
    per = FN_BLOCKS if (seq // BLK) % FN_BLOCKS == 0 else 1
    in_specs = [pl.BlockSpec((BLK, D_MODEL), lambda b, j, r=r: (start_block + b * nb + 1 + j * per + r, 0))
                for r in range(per)]
    return pl.pallas_call(
        _final_norm_body,
        grid=(batch, seq // (per * BLK)),
        in_specs=in_specs + [pl.BlockSpec((1, D_MODEL), lambda b, j: (0, 0))],
        out_specs=pl.BlockSpec((None, per * BLK, D_MODEL), lambda b, j: (b, j, 0)),
        out_shape=jax.ShapeDtypeStruct((batch, seq, D_MODEL), F32),
        compiler_params=_cparams("parallel", "parallel"),
        name="final_norm",
    )(*([h] * per), w)


def _regroup_body(perm_ref, wt_ref, o_ref):
    o_ref[...] = wt_ref[...].T.astype(BF16)


def _regroup_w_in(w_in_t, perm):
    depth = w_in_t.shape[0]
    grid_spec = pltpu.PrefetchScalarGridSpec(
        num_scalar_prefetch=1,
        grid=(depth, N_BIG // W_COLS),
        in_specs=[pl.BlockSpec((None, W_COLS, D_MODEL), lambda l, j, perm: (l, perm[j], 0))],
        out_specs=pl.BlockSpec((None, D_MODEL, W_COLS), lambda l, j, perm: (l, 0, j)),
    )
    return pl.pallas_call(
        _regroup_body,
        grid_spec=grid_spec,
        out_shape=jax.ShapeDtypeStruct((depth, D_MODEL, N_BIG), BF16),
        compiler_params=_cparams("parallel", "parallel"),
        name="regroup_w_in",
    )(perm, w_in_t)


def _gate_weights_body(wt_ref, wst_ref):
    x = wt_ref[:BLK, :]
    wst_ref[...] = jnp.where(lax.broadcasted_iota(jnp.int32, x.shape, 0) < N_GATE, x, 0.0).astype(BF16)


def _gate_weights(w_in_t):
    depth = w_in_t.shape[0]
    return pl.pallas_call(
        _gate_weights_body,
        grid=(depth,),
        in_specs=[pl.BlockSpec((None, W_COLS, D_MODEL), lambda l: (l, N_BIG // W_COLS, 0))],
        out_specs=pl.BlockSpec((None, BLK, D_MODEL), lambda l: (l, 0, 0)),
        out_shape=jax.ShapeDtypeStruct((depth, BLK, D_MODEL), BF16),
        compiler_params=_cparams("parallel"),
        name="gate_weights",
    )(w_in_t)


def _tables(seq_blocks, n_blocks):
    pos, nblk, fblk, bblk, first = [], [], [], [], []
    start = 0
    for nb in seq_blocks:
        pos += list(range(nb))
        nblk += [nb] * nb
        fblk += [start + j for j in range(nb)]
        bblk += [start + nb - 1 - j for j in range(nb)]
        first += [1] + [0] * (nb - 1)
        start += nb
    pos += [0] * (n_blocks - start)
    nblk += [1] * (n_blocks - start)
    as_i32 = lambda a: jnp.asarray(np.asarray(a, np.int32))
    return as_i32(pos), as_i32(nblk), as_i32(fblk), as_i32(bblk), as_i32(first)


def _column_perm():
    c = np.cumsum([0, CONV_W, CONV_W, CONV_W, CONV_W, ATT_Q, ATT_KVW, ATT_KVW, ATT_Q,
                   DN_W, DN_W, DN_W, DN_W]) // W_COLS
    blocks = lambda a: np.arange(c[a], c[a + 1])
    cx, cb, cc, cz, aq, ak, av, az, dq, dk, dv, dz = (blocks(a) for a in range(12))
    halves = CONV_HALF // W_COLS
    conv = [g[s * halves:(s + 1) * halves] for s in range(CONV_W // CONV_HALF) for g in (cx, cb, cc, cz)]
    return np.concatenate([dq, dk, dv, dz, aq, az, ak, av] + conv).astype(np.int32)


def _forward(xs, meta_tokens, norm_w, w_in, conv_a_w, attn_sink, dn_conv_w, dn_a_log, dn_dt_bias,
             dn_norm_w, w_out, final_norm_w):
    depth = w_in.shape[0]
    lead = jnp.concatenate([jnp.zeros((PAD, D_MODEL), F32), meta_tokens.astype(F32)], axis=0)
    rows, seq_blocks = [], []
    for x in xs:
        for bi in range(x.shape[0]):
            rows += [lead, x[bi]]
            seq_blocks.append((BLK + x.shape[1]) // BLK)
    n_real = sum(seq_blocks) * BLK
    n = -(-n_real // TM_PROJ) * TM_PROJ
    rows.append(jnp.zeros((n - n_real, D_MODEL), F32))
    h = jnp.concatenate(rows, axis=0)

    valid_np = np.zeros((n, 1), np.float32)
    start = 0
    for nb in seq_blocks:
        valid_np[start + PAD:start + nb * BLK] = 1.0
        start += nb * BLK
    valid = jnp.asarray(valid_np)
    pos_tab, nblk_tab, fblk, bblk, first = _tables(seq_blocks, n // BLK)

    w_in_t = jnp.swapaxes(w_in, 1, 2)
    w_big = _regroup_w_in(w_in_t, jnp.asarray(_column_perm()))
    w_small_t = _gate_weights(w_in_t)
    pad_gate = lambda a: jnp.pad(a.reshape(depth, 2 * DN_H), ((0, 0), (2 * DN_H, BLK - N_GATE)))[:, :, None]
    a_col, b_col = pad_gate(dn_a_log.astype(F32)), pad_gate(dn_dt_bias.astype(F32))
    dn_conv_w4 = jnp.pad(jnp.swapaxes(dn_conv_w.reshape(depth, 3, 3, DN_W), 1, 2),
                         ((0, 0), (0, 1), (0, 0), (0, 0)))
    w_out_b = w_out.astype(BF16)
    norm_w3 = norm_w[:, None, :]
    dn_norm_w3 = dn_norm_w[:, None, :]

    for l in range(depth):
        qkvz, proj_a, gates_t, xn = _proj_main(h, norm_w3, w_big, w_small_t, a_col, b_col,
                                               dn_conv_w4, valid, l)
        y_conv = _proj_conv(xn, w_big, conv_a_w, l)
        y_att = _attention(proj_a, attn_sink, pos_tab, nblk_tab, l)
        o_f, o_b = _dn_scan(qkvz, gates_t, fblk, bblk, first)
        h = _out_proj(h, y_conv, y_att, o_f, o_b, qkvz, dn_norm_w3, valid, w_out_b, l)

    ys, start = [], 0
    for x in xs:
        b, s = x.shape[0], x.shape[1]
        ys.append(_final_norm(h, final_norm_w[None, :], start, b, s))
        start += b * (BLK + s) // BLK
    return tuple(ys)


def kernel(x_prompt, x_sample, meta_tokens, norm_w, w_in, conv_a_w, attn_sink, dn_conv_w,
           dn_a_log, dn_dt_bias, dn_norm_w, w_out, final_norm_w):
    return _forward([x_prompt, x_sample], meta_tokens, norm_w, w_in, conv_a_w, attn_sink, dn_conv_w,
                    dn_a_log, dn_dt_bias, dn_norm_w, w_out, final_norm_w)
```

```python
import numpy as np
import jax
import jax.numpy as jnp
from jax import lax
from jax.experimental import pallas as pl
from jax.experimental.pallas import tpu as pltpu

F32 = jnp.float32
BF16 = jnp.bfloat16
HIGHEST = lax.Precision.HIGHEST

D_MODEL = 2048
N_META = 16
BLK = 128
PAD = BLK - N_META
CONV_W = 512
HD = 128
ATT_HQ = 6
ATT_KV = 2
ATT_G = ATT_HQ // ATT_KV
ATT_Q = ATT_HQ * HD
ATT_KVW = ATT_KV * HD
DN_H = 6
DN_W = DN_H * HD
N_GATE = 4 * DN_H
EPS = 1e-6

N_DN = 4 * DN_W
N_ATT = 2 * ATT_Q + 2 * ATT_KVW
N_CONV = 4 * CONV_W
N_BIG = N_DN + N_ATT + N_CONV
CONV_HALF = CONV_W // 2
W_COLS = 256

TM_PROJ = 1024
TN_PROJ = 1024
ROW_CHUNK = 256
TM = 512
HALO = 8
HALO_IN = 16
VMEM_LIMIT = 56 * 1024 * 1024


def _cparams(*sem):
    return pltpu.CompilerParams(dimension_semantics=sem, vmem_limit_bytes=VMEM_LIMIT)


def _silu(x):
    return x * (1.0 / (1.0 + jnp.exp(-x)))


def _dot(a, b):
    return jnp.dot(a, b, preferred_element_type=F32)


def _dot_nt(a, b, precision=None):
    return lax.dot_general(a, b, (((1,), (1,)), ((), ())), preferred_element_type=F32,
                           precision=precision)


def _softplus(x):
    return jnp.maximum(x, 0.0) + jnp.log1p(jnp.exp(-jnp.abs(x)))


def _gate_fn(x, idx, a_log, bias):
    beta = 1.0 / (1.0 + jnp.exp(-x))
    g = -jnp.exp(a_log) * _softplus(x + bias)
    return jnp.where(idx < 2 * DN_H, beta, jnp.where(idx < N_GATE, g, 0.0))


def _halo_specs(n):
    per = TM_PROJ // HALO_IN
    last = n // HALO_IN - 1
    return [pl.BlockSpec((TM_PROJ, D_MODEL), lambda i, j: (i, 0)),
            pl.BlockSpec((HALO_IN, D_MODEL), lambda i, j: (jnp.maximum(i * per - 1, 0), 0)),
            pl.BlockSpec((HALO_IN, D_MODEL), lambda i, j: (jnp.minimum((i + 1) * per, last), 0))]


def _edge_flags():
    i = pl.program_id(0)
    return (i > 0).astype(F32), (i < pl.num_programs(0) - 1).astype(F32)


def _chunked(compute, epilogue):
    chunks = TM_PROJ // ROW_CHUNK
    for c in range(chunks):
        compute(c)
        if c > 0:
            epilogue(c - 1)
    epilogue(chunks - 1)


def _proj_main_body(h_ref, hp_ref, hn_ref, nw_ref, wdn_ref, watt_ref, wst_ref, acol_ref, bcol_ref,
                    cw_ref, valid_ref, o_ref, pa_ref, gt_ref, xn_ref, acc_ref, xh_ref):
    j = pl.program_id(1)

    def rmsnorm(x):
        ms = jnp.mean(x * x, axis=-1, keepdims=True)
        return (x * lax.rsqrt(ms + EPS) * nw_ref[...]).astype(BF16)

    @pl.when(j == 0)
    def _():
        for r in range(0, TM_PROJ, ROW_CHUNK):
            xn_ref[r:r + ROW_CHUNK, :] = rmsnorm(h_ref[r:r + ROW_CHUNK, :])
        xh_ref[0:HALO_IN, :] = rmsnorm(hp_ref[...])
        xh_ref[HALO_IN:, :] = rmsnorm(hn_ref[...])
        pt = _dot_nt(wst_ref[...], xn_ref[...])
        gt_ref[...] = _gate_fn(pt, lax.broadcasted_iota(jnp.int32, pt.shape, 0), acol_ref[...], bcol_ref[...])

    has_prev, has_next = _edge_flags()
    acc_ref[0:HALO, :] = _dot(xh_ref[0:HALO_IN, :], wdn_ref[...])[HALO_IN - HALO:, :] * has_prev
    acc_ref[HALO + TM_PROJ:, :] = _dot(xh_ref[HALO_IN:, :], wdn_ref[...])[:HALO, :] * has_next
    is_gate = j == 3
    q_scale = jnp.where(j == 0, HD ** -0.5, 1.0)

    def compute(c):
        r0 = c * ROW_CHUNK
        acc_ref[HALO + r0:HALO + r0 + ROW_CHUNK, :] = _dot(xn_ref[r0:r0 + ROW_CHUNK, :], wdn_ref[...])

    def attention_half(half):
        r0 = half * (TM_PROJ // 2)
        pa_ref[r0:r0 + TM_PROJ // 2, :] = _dot(xn_ref[r0:r0 + TM_PROJ // 2, :], watt_ref[...]).astype(BF16)

    def epilogue(c):
        r0 = c * ROW_CHUNK
        x = acc_ref[HALO + r0:HALO + r0 + ROW_CHUNK, :]
        x_m1 = acc_ref[HALO + r0 - 1:HALO + r0 - 1 + ROW_CHUNK, :]
        x_p1 = acc_ref[HALO + r0 + 1:HALO + r0 + 1 + ROW_CHUNK, :]
        conv = x_m1 * cw_ref[0:1, :] + x * cw_ref[1:2, :] + x_p1 * cw_ref[2:3, :]
        y = _silu(jnp.where(is_gate, x, conv))
        row_f = jnp.where(j == 1, valid_ref[r0:r0 + ROW_CHUNK, :], 1.0) * q_scale
        for hh in range(DN_H):
            hs = slice(hh * HD, (hh + 1) * HD)
            yh = y[:, hs]
            inv = lax.rsqrt(jnp.sum(yh * yh, axis=-1, keepdims=True) + EPS)
            o_ref[r0:r0 + ROW_CHUNK, hs] = (yh * (jnp.where(j < 2, inv, 1.0) * row_f)).astype(BF16)

    chunks = TM_PROJ // ROW_CHUNK
    for c in range(chunks):
        compute(c)
        if c > 0:
            epilogue(c - 1)
        if c == chunks // 2 - 1:
            attention_half(0)
    attention_half(1)
    epilogue(chunks - 1)


TN_ATT = N_ATT // (N_DN // DN_W)


def _proj_main(h, norm_w, w_big, w_small_t, acol, bcol, dn_conv_w4, valid, layer):
    n = h.shape[0]
    lay3 = lambda i, j: (layer, 0, 0)
    return pl.pallas_call(
        _proj_main_body,
        grid=(n // TM_PROJ, N_DN // DN_W),
        in_specs=_halo_specs(n) + [
            pl.BlockSpec((None, 1, D_MODEL), lay3),
            pl.BlockSpec((None, D_MODEL, DN_W), lambda i, j: (layer, 0, j)),
            pl.BlockSpec((None, D_MODEL, TN_ATT), lambda i, j: (layer, 0, N_DN // TN_ATT + j)),
            pl.BlockSpec((None, BLK, D_MODEL), lay3),
            pl.BlockSpec((None, BLK, 1), lay3),
            pl.BlockSpec((None, BLK, 1), lay3),
            pl.BlockSpec((None, None, 3, DN_W), lambda i, j: (layer, j, 0, 0)),
            pl.BlockSpec((TM_PROJ, 1), lambda i, j: (i, 0)),
        ],
        out_specs=[pl.BlockSpec((None, TM_PROJ, DN_W), lambda i, j: (j, i, 0)),
                   pl.BlockSpec((TM_PROJ, TN_ATT), lambda i, j: (i, j)),
                   pl.BlockSpec((BLK, TM_PROJ), lambda i, j: (0, i)),
                   pl.BlockSpec((TM_PROJ, D_MODEL), lambda i, j: (i, 0))],
        out_shape=[jax.ShapeDtypeStruct((N_DN // DN_W, n, DN_W), BF16),
                   jax.ShapeDtypeStruct((n, N_ATT), BF16),
                   jax.ShapeDtypeStruct((BLK, n), F32),
                   jax.ShapeDtypeStruct((n, D_MODEL), BF16)],
        scratch_shapes=[pltpu.VMEM((TM_PROJ + 2 * HALO, DN_W), F32),
                        pltpu.VMEM((2 * HALO_IN, D_MODEL), BF16)],
        compiler_params=_cparams("parallel", "arbitrary"),
        name="proj_main",
    )(h, h, h, norm_w, w_big, w_big, w_small_t, acol, bcol, dn_conv_w4, valid)


def _proj_conv_body(xn_ref, prev_ref, next_ref, w_ref, cw_ref, o_ref, u_ref, bz_ref):
    cx, cb, cc, cz = (slice(k * CONV_HALF, (k + 1) * CONV_HALF) for k in range(4))
    has_prev, has_next = _edge_flags()
    pv = _dot(prev_ref[...], w_ref[...])[HALO_IN - HALO:, :]
    nx = _dot(next_ref[...], w_ref[...])[:HALO, :]
    u_ref[0:HALO, :] = pv[:, cc] * pv[:, cx] * has_prev
    u_ref[HALO + TM_PROJ:, :] = nx[:, cc] * nx[:, cx] * has_next

    def compute(c):
        r0 = c * ROW_CHUNK
        acc = _dot(xn_ref[r0:r0 + ROW_CHUNK, :], w_ref[...])
        u_ref[HALO + r0:HALO + r0 + ROW_CHUNK, :] = acc[:, cc] * acc[:, cx]
        bz_ref[r0:r0 + ROW_CHUNK, :] = acc[:, cb] * _silu(acc[:, cz])

    def epilogue(c):
        r0 = c * ROW_CHUNK
        u = u_ref[HALO + r0:HALO + r0 + ROW_CHUNK, :]
        u_m1 = u_ref[HALO + r0 - 1:HALO + r0 - 1 + ROW_CHUNK, :]
        u_p1 = u_ref[HALO + r0 + 1:HALO + r0 + 1 + ROW_CHUNK, :]
        conv = u_m1 * cw_ref[0:1, :] + u * cw_ref[1:2, :] + u_p1 * cw_ref[2:3, :]
        o_ref[r0:r0 + ROW_CHUNK, :] = (bz_ref[r0:r0 + ROW_CHUNK, :] * conv).astype(BF16)

    _chunked(compute, epilogue)


def _proj_conv(xn, w_big, conv_w, layer):
    n = xn.shape[0]
    first = (N_DN + N_ATT) // TN_PROJ
    return pl.pallas_call(
        _proj_conv_body,
        grid=(n // TM_PROJ, N_CONV // TN_PROJ),
        in_specs=_halo_specs(n) + [
            pl.BlockSpec((None, D_MODEL, TN_PROJ), lambda i, j: (layer, 0, first + j)),
            pl.BlockSpec((None, 3, CONV_HALF), lambda i, j: (layer, 0, j)),
        ],
        out_specs=pl.BlockSpec((TM_PROJ, CONV_HALF), lambda i, j: (i, j)),
        out_shape=jax.ShapeDtypeStruct((n, CONV_W), BF16),
        scratch_shapes=[pltpu.VMEM((TM_PROJ + 2 * HALO, CONV_HALF), F32),
                        pltpu.VMEM((TM_PROJ, CONV_HALF), F32)],
        compiler_params=_cparams("parallel", "parallel"),
        name="proj_conv",
    )(xn, xn, xn, w_big, conv_w)


ATT_QB = 8
CB_AQ, CB_AZ = 0, 1
CB_AK, CB_AV = 2 * ATT_Q // ATT_KVW, 2 * ATT_Q // ATT_KVW + 1


def _band(prev_ref, cur_ref, next_ref, qb, j, cols):
    blocks = ([prev_ref[:, cols]] + [cur_ref[r * BLK:(r + 1) * BLK, cols] for r in range(qb)]
              + [next_ref[:, cols]])
    return jnp.concatenate(blocks[j:j + 3], axis=0)


def _attn_probs(pos_ref, nblk_ref, sink_ref, first_block, qb, q_ref, kp, kc, kn):
    qi = lax.broadcasted_iota(jnp.int32, (BLK, 3 * BLK), 0)
    kj = lax.broadcasted_iota(jnp.int32, (BLK, 3 * BLK), 1) - BLK
    dist = jnp.abs(qi - kj)
    distf = dist.astype(F32)
    scores = {}
    for j in range(qb):
        rows = slice(j * BLK, (j + 1) * BLK)
        for kvh in range(ATT_KV):
            k3 = _band(kp, kc, kn, qb, j, slice(kvh * HD, (kvh + 1) * HD))
            q3 = jnp.concatenate([q_ref[rows, (kvh * ATT_G + g) * HD:(kvh * ATT_G + g + 1) * HD]
                                  for g in range(ATT_G)], axis=0)
            scores[j, kvh] = _dot_nt(q3, k3)
    probs, dens = {}, {}
    for j in range(qb):
        b = first_block + j
        kabs = pos_ref[b] * BLK + kj
        allowed = (dist <= BLK) & (kabs >= PAD) & (kabs < nblk_ref[b] * BLK)
        for head in range(ATT_HQ):
            kvh, g = divmod(head, ATT_G)
            slope = float(2.0 ** (-8.0 * (head + 1) / ATT_HQ))
            s = scores[j, kvh][g * BLK:(g + 1) * BLK] * (HD ** -0.5) - slope * distf
            s = jnp.where(allowed, s, -jnp.inf)
            sk = sink_ref[head]
            m = jnp.maximum(jnp.max(s, axis=-1, keepdims=True), sk)
            p = jnp.exp(s - m)
            dens[j, head] = jnp.sum(p, axis=-1, keepdims=True) + jnp.exp(sk - m)
            probs[j, head] = p.astype(BF16)
    return probs, dens


def _attn_values(probs, dens, qb, az_ref, vp, vc, vn, o_ref):
    for j in range(qb):
        rows = slice(j * BLK, (j + 1) * BLK)
        for kvh in range(ATT_KV):
            v3 = _band(vp, vc, vn, qb, j, slice(kvh * HD, (kvh + 1) * HD))
            p3 = jnp.concatenate([probs[j, kvh * ATT_G + g] for g in range(ATT_G)], axis=0)
            o3 = _dot(p3, v3)
            for g in range(ATT_G):
                head = kvh * ATT_G + g
                hs = slice(head * HD, (head + 1) * HD)
                o = o3[g * BLK:(g + 1) * BLK] / dens[j, head]
                o_ref[rows, hs] = (o * _silu(az_ref[rows, hs].astype(F32))).astype(BF16)


def _attn_body(pos_ref, nblk_ref, sink_ref, q_ref, az_ref, kp, kc, kn, vp, vc, vn, o_ref):
    first_block = pl.program_id(0) * ATT_QB
    probs, dens = _attn_probs(pos_ref, nblk_ref, sink_ref, first_block, ATT_QB, q_ref, kp, kc, kn)
    _attn_values(probs, dens, ATT_QB, az_ref, vp, vc, vn, o_ref)


def _attention(proj, sink, pos_tab, nblk_tab, layer):
    n = proj.shape[0]
    nb = n // BLK
    tq = ATT_QB * BLK
    prev = lambda i, *_: (jnp.maximum(i * ATT_QB - 1, 0),)
    nxt = lambda i, *_: (jnp.minimum((i + 1) * ATT_QB, nb - 1),)
    kv_specs = []
    for c in (CB_AK, CB_AV):
        kv_specs += [pl.BlockSpec((BLK, ATT_KVW), lambda i, *_, c=c: prev(i) + (c,)),
                     pl.BlockSpec((tq, ATT_KVW), lambda i, *_, c=c: (i, c)),
                     pl.BlockSpec((BLK, ATT_KVW), lambda i, *_, c=c: nxt(i) + (c,))]
    grid_spec = pltpu.PrefetchScalarGridSpec(
        num_scalar_prefetch=2,
        grid=(nb // ATT_QB,),
        in_specs=[
            pl.BlockSpec(memory_space=pltpu.SMEM),
            pl.BlockSpec((tq, ATT_Q), lambda i, *_: (i, CB_AQ)),
            pl.BlockSpec((tq, ATT_Q), lambda i, *_: (i, CB_AZ)),
        ] + kv_specs,
        out_specs=pl.BlockSpec((tq, ATT_Q), lambda i, *_: (i, 0)),
    )
    return pl.pallas_call(
        _attn_body,
        grid_spec=grid_spec,
        out_shape=jax.ShapeDtypeStruct((n, ATT_Q), BF16),
        compiler_params=_cparams("parallel"),
        name="attention",
    )(pos_tab, nblk_tab, sink[layer], proj, proj, proj, proj, proj, proj, proj, proj)


def _dn_chunks(chains, r_i, c_i, s_ref):
    xor_idx = r_i ^ c_i
    st = []
    for q, k, v, beta, col, row, tot, lower, idx in chains:
        incl, strict = (r_i >= c_i, r_i > c_i) if lower else (r_i <= c_i, r_i < c_i)
        decay = jnp.exp(jnp.where(incl, col - row, -jnp.inf))
        kbeta = k * beta
        gram = _dot_nt(jnp.concatenate([kbeta, q], axis=0).astype(BF16), k.astype(BF16))
        p = -jnp.where(strict, gram[:BLK] * decay, 0.0)
        ecol = jnp.exp(col)
        st.append(dict(
            p=p, idx=idx, tot=tot, qk=(gram[BLK:] * decay).astype(BF16),
            rhs=jnp.concatenate([v * beta, kbeta * ecol], axis=1).astype(BF16),
            qg=(q * ecol).astype(BF16),
            k_tail_t=(k * jnp.exp(tot - col)).T.astype(BF16),
            t=(xor_idx == 0).astype(F32) + jnp.where(xor_idx == 1, p, 0.0)))
    s = 2
    while s < BLK:
        for c in st:
            c["tb"] = c["t"].astype(BF16)
            link = jnp.where((xor_idx >= s) & (xor_idx < 2 * s), c["p"], 0.0).astype(BF16)
            c["tl"] = _dot(c["tb"], link).astype(BF16)
        for c in st:
            c["t"] = c["t"] + _dot(c["tl"], c["tb"])
        s *= 2
    for c in st:
        c["x"] = _dot(c["t"].astype(BF16), c["rhs"])
    for c in st:
        c["s"] = s_ref[c["idx"]]
        c["ws"] = _dot(jnp.concatenate([c["x"][:, HD:].astype(BF16), c["qg"]], axis=0),
                       c["s"].astype(BF16))
    for c in st:
        v_new = c["x"][:, :HD] - c["ws"][:BLK]
        c["r"] = _dot(jnp.concatenate([c["qk"], c["k_tail_t"]], axis=0), v_new.astype(BF16))
    outs = []
    for c in st:
        s_ref[c["idx"]] = c["s"] * jnp.exp(c["tot"]) + c["r"][BLK:]
        outs.append(c["ws"][BLK:] + c["r"][:BLK])
    return outs


def _dn_scan_body(fblk_ref, bblk_ref, first_ref,
                  qf, kf, vf, gtf, qb, kb, vb, gtb, of_ref, ob_ref, s_ref):
    i = pl.program_id(0)

    @pl.when(first_ref[i] == 1)
    def _():
        s_ref[...] = jnp.zeros_like(s_ref)

    r_i = lax.broadcasted_iota(jnp.int32, (BLK, BLK), 0)
    c_i = lax.broadcasted_iota(jnp.int32, (BLK, BLK), 1)
    lower = (r_i >= c_i).astype(F32)
    upper = (r_i <= c_i).astype(F32)
    chains, dests = [], []
    for d, (q_ref, k_ref, v_ref, gt_ref, o_ref) in enumerate(
            ((qf, kf, vf, gtf, of_ref), (qb, kb, vb, gtb, ob_ref))):
        csum_t, last = (upper, BLK - 1) if d == 0 else (lower, 0)
        gates_t = gt_ref[...]
        gct = jnp.dot(gates_t, csum_t, preferred_element_type=F32, precision=HIGHEST)
        gates, gc = gates_t.T, gct.T
        for hh in range(DN_H):
            hs = slice(hh * HD, (hh + 1) * HD)
            cb = d * DN_H + hh
            cg = 2 * DN_H + cb
            chains.append((q_ref[:, hs].astype(F32), k_ref[:, hs].astype(F32), v_ref[:, hs].astype(F32),
                           gates[:, cb:cb + 1], gc[:, cg:cg + 1], gct[cg:cg + 1, :],
                           gc[last:last + 1, cg:cg + 1], d == 0, cb))
            dests.append((o_ref, hs))
    for (o_ref, hs), o in zip(dests, _dn_chunks(chains, r_i, c_i, s_ref)):
        o_ref[:, hs] = o.astype(BF16)


def _dn_scan(qkvz, gates_t, fblk, bblk, first):
    n = qkvz.shape[1]
    steps = fblk.shape[0]
    fmap = lambda i, f, b, s: (f[i], 0)
    bmap = lambda i, f, b, s: (b[i], 0)
    fmap_t = lambda i, f, b, s: (0, f[i])
    bmap_t = lambda i, f, b, s: (0, b[i])
    fwide = [pl.BlockSpec((None, BLK, DN_W), lambda i, f, b, s, a=a: (a, f[i], 0)) for a in range(3)]
    bwide = [pl.BlockSpec((None, BLK, DN_W), lambda i, f, b, s, a=a: (a, b[i], 0)) for a in range(3)]
    grid_spec = pltpu.PrefetchScalarGridSpec(
        num_scalar_prefetch=3,
        grid=(steps,),
        in_specs=fwide + [pl.BlockSpec((BLK, BLK), fmap_t)] + bwide + [pl.BlockSpec((BLK, BLK), bmap_t)],
        out_specs=[pl.BlockSpec((BLK, DN_W), fmap), pl.BlockSpec((BLK, DN_W), bmap)],
        scratch_shapes=[pltpu.VMEM((2 * DN_H, HD, HD), F32)],
    )
    out = jax.ShapeDtypeStruct((n, DN_W), BF16)
    return pl.pallas_call(
        _dn_scan_body,
        grid_spec=grid_spec,
        out_shape=[out, out],
        compiler_params=_cparams("arbitrary"),
        name="dn_scan",
    )(fblk, bblk, first, qkvz, qkvz, qkvz, gates_t, qkvz, qkvz, qkvz, gates_t)


def _out_proj_body(h_ref, yc_ref, ya_ref, of_ref, ob_ref, zs_ref, nw_ref, valid_ref, w_ref, o_ref):
    parts = []
    for hh in range(DN_H):
        hs = slice(hh * HD, (hh + 1) * HD)
        o = of_ref[:, hs].astype(F32) + ob_ref[:, hs].astype(F32)
        ms = jnp.mean(o * o, axis=-1, keepdims=True)
        on = o * lax.rsqrt(ms + EPS) * nw_ref[...]
        parts.append((on * zs_ref[:, hs].astype(F32)).astype(BF16))
    y_dn = jnp.concatenate(parts, axis=1)
    c0, c1 = CONV_W, CONV_W + ATT_Q
    acc = (_dot(yc_ref[...], w_ref[0:c0, :]) + _dot(ya_ref[...], w_ref[c0:c1, :])
           + _dot(y_dn, w_ref[c1:, :]))
    o_ref[...] = h_ref[...] + jnp.where(valid_ref[...] > 0.0, acc, 0.0)


def _out_proj(h, y_conv, y_att, o_f, o_b, qkvz, dn_norm_w, valid, w_out, layer):
    n = h.shape[0]
    row = lambda width: pl.BlockSpec((TM, width), lambda i: (i, 0))
    return pl.pallas_call(
        _out_proj_body,
        grid=(n // TM,),
        in_specs=[row(D_MODEL), row(CONV_W), row(ATT_Q), row(DN_W), row(DN_W),
                  pl.BlockSpec((None, TM, DN_W), lambda i: (3, i, 0)),
                  pl.BlockSpec((None, 1, HD), lambda i: (layer, 0, 0)),
                  row(1),
                  pl.BlockSpec((None, D_MODEL, D_MODEL), lambda i: (layer, 0, 0))],
        out_specs=row(D_MODEL),
        out_shape=jax.ShapeDtypeStruct((n, D_MODEL), F32),
        compiler_params=_cparams("parallel"),
        name="out_proj",
    )(h, y_conv, y_att, o_f, o_b, qkvz, dn_norm_w, valid, w_out)


FN_BLOCKS = 8


def _final_norm_body(*refs):
    *h_refs, nw_ref, o_ref = refs
    for r, h_ref in enumerate(h_refs):
        x = h_ref[...]
        ms = jnp.mean(x * x, axis=-1, keepdims=True)
        o_ref[r * BLK:(r + 1) * BLK, :] = x * lax.rsqrt(ms + EPS) * nw_ref[...]


def _final_norm(h, w, start_block, batch, seq):
    nb = (BLK + seq) // BLK
    per = FN_BLOCKS if (seq // BLK) % FN_BLOCKS == 0 else 1
    in_specs = [pl.BlockSpec((BLK, D_MODEL), lambda b, j, r=r: (start_block + b * nb + 1 + j * per + r, 0))
                for r in range(per)]
    return pl.pallas_call(
        _final_norm_body,
        grid=(batch, seq // (per * BLK)),
        in_specs=in_specs + [pl.BlockSpec((1, D_MODEL), lambda b, j: (0, 0))],
        out_specs=pl.BlockSpec((None, per * BLK, D_MODEL), lambda b, j: (b, j, 0)),
        out_shape=jax.ShapeDtypeStruct((batch, seq, D_MODEL), F32),
        compiler_params=_cparams("parallel", "parallel"),
        name="final_norm",
    )(*([h] * per), w)


def _regroup_body(perm_ref, wt_ref, o_ref):
    o_ref[...] = wt_ref[...].T.astype(BF16)


def _regroup_w_in(w_in_t, perm):
    depth = w_in_t.shape[0]
    grid_spec = pltpu.PrefetchScalarGridSpec(
        num_scalar_prefetch=1,
        grid=(depth, N_BIG // W_COLS),
        in_specs=[pl.BlockSpec((None, W_COLS, D_MODEL), lambda l, j, perm: (l, perm[j], 0))],
        out_specs=pl.BlockSpec((None, D_MODEL, W_COLS), lambda l, j, perm: (l, 0, j)),
    )
    return pl.pallas_call(
        _regroup_body,
        grid_spec=grid_spec,
        out_shape=jax.ShapeDtypeStruct((depth, D_MODEL, N_BIG), BF16),
        compiler_params=_cparams("parallel", "parallel"),
        name="regroup_w_in",
    )(perm, w_in_t)


def _gate_weights_body(wt_ref, wst_ref):
    x = wt_ref[:BLK, :]
    wst_ref[...] = jnp.where(lax.broadcasted_iota(jnp.int32, x.shape, 0) < N_GATE, x, 0.0).astype(BF16)


def _gate_weights(w_in_t):
    depth = w_in_t.shape[0]
    return pl.pallas_call(
        _gate_weights_body,
        grid=(depth,),
        in_specs=[pl.BlockSpec((None, W_COLS, D_MODEL), lambda l: (l, N_BIG // W_COLS, 0))],
        out_specs=pl.BlockSpec((None, BLK, D_MODEL), lambda l: (l, 0, 0)),
        out_shape=jax.ShapeDtypeStruct((depth, BLK, D_MODEL), BF16),
        compiler_params=_cparams("parallel"),
        name="gate_weights",
    )(w_in_t)


def _tables(seq_blocks, n_blocks):
    pos, nblk, fblk, bblk, first = [], [], [], [], []
    start = 0
    for nb in seq_blocks:
        pos += list(range(nb))
        nblk += [nb] * nb
        fblk += [start + j for j in range(nb)]
        bblk += [start + nb - 1 - j for j in range(nb)]
        first += [1] + [0] * (nb - 1)
        start += nb
    pos += [0] * (n_blocks - start)
    nblk += [1] * (n_blocks - start)
    as_i32 = lambda a: jnp.asarray(np.asarray(a, np.int32))
    return as_i32(pos), as_i32(nblk), as_i32(fblk), as_i32(bblk), as_i32(first)


def _column_perm():
    c = np.cumsum([0, CONV_W, CONV_W, CONV_W, CONV_W, ATT_Q, ATT_KVW, ATT_KVW, ATT_Q,
                   DN_W, DN_W, DN_W, DN_W]) // W_COLS
    blocks = lambda a: np.arange(c[a], c[a + 1])
    cx, cb, cc, cz, aq, ak, av, az, dq, dk, dv, dz = (blocks(a) for a in range(12))
    halves = CONV_HALF // W_COLS
    conv = [g[s * halves:(s + 1) * halves] for s in range(CONV_W // CONV_HALF) for g in (cx, cb, cc, cz)]
    return np.concatenate([dq, dk, dv, dz, aq, az, ak, av] + conv).astype(np.int32)


def _forward(xs, meta_tokens, norm_w, w_in, conv_a_w, attn_sink, dn_conv_w, dn_a_log, dn_dt_bias,
             dn_norm_w, w_out, final_norm_w):
    depth = w_in.shape[0]
    lead = jnp.concatenate([jnp.zeros((PAD, D_MODEL), F32), meta_tokens.astype(F32)], axis=0)
    rows, seq_blocks = [], []
    for x in xs:
        for bi in range(x.shape[0]):
            rows += [lead, x[bi]]
            seq_blocks.append((BLK + x.shape[1]) // BLK)
    n_real = sum(seq_blocks) * BLK
    n = -(-n_real // TM_PROJ) * TM_PROJ
    rows.append(jnp.zeros((n - n_real, D_MODEL), F32))
    h = jnp.concatenate(rows, axis=0)

    valid_np = np.zeros((n, 1), np.float32)
    start = 0
    for nb in seq_blocks:
        valid_np[start + PAD:start + nb * BLK] = 1.0
        start += nb * BLK
    valid = jnp.asarray(valid_np)
    pos_tab, nblk_tab, fblk, bblk, first = _tables(seq_blocks, n // BLK)

    w_in_t = jnp.swapaxes(w_in, 1, 2)
    w_big = _regroup_w_in(w_in_t, jnp.asarray(_column_perm()))
    w_small_t = _gate_weights(w_in_t)
    pad_gate = lambda a: jnp.pad(a.reshape(depth, 2 * DN_H), ((0, 0), (2 * DN_H, BLK - N_GATE)))[:, :, None]
    a_col, b_col = pad_gate(dn_a_log.astype(F32)), pad_gate(dn_dt_bias.astype(F32))
    dn_conv_w4 = jnp.pad(jnp.swapaxes(dn_conv_w.reshape(depth, 3, 3, DN_W), 1, 2),
                         ((0, 0), (0, 1), (0, 0), (0, 0)))
    w_out_b = w_out.astype(BF16)
    norm_w3 = norm_w[:, None, :]
    dn_norm_w3 = dn_norm_w[:, None, :]

    for l in range(depth):
        qkvz, proj_a, gates_t, xn = _proj_main(h, norm_w3, w_big, w_small_t, a_col, b_col,
                                               dn_conv_w4, valid, l)
        y_conv = _proj_conv(xn, w_big, conv_a_w, l)
        y_att = _attention(proj_a, attn_sink, pos_tab, nblk_tab, l)
        o_f, o_b = _dn_scan(qkvz, gates_t, fblk, bblk, first)
        h = _out_proj(h, y_conv, y_att, o_f, o_b, qkvz, dn_norm_w3, valid, w_out_b, l)

    ys, start = [], 0
    for x in xs:
        b, s = x.shape[0], x.shape[1]
        ys.append(_final_norm(h, final_norm_w[None, :], start, b, s))
        start += b * (BLK + s) // BLK
    return tuple(ys)


def kernel(x_prompt, x_sample, meta_tokens, norm_w, w_in, conv_a_w, attn_sink, dn_conv_w,
           dn_a_log, dn_dt_bias, dn_norm_w, w_out, final_norm_w):
    return _forward([x_prompt, x_sample], meta_tokens, norm_w, w_in, conv_a_w, attn_sink, dn_conv_w,
                    dn_a_log, dn_dt_bias, dn_norm_w, w_out, final_norm_w)
```
